```python
import math
import jax, jax.numpy as jnp
from jax import lax
import numpy as np

D_MODEL = 1024
BATCH = 8
SEQ = 8192
DEPTH = 1
DEC_BATCH = 8
DEC_SEQ = 32
PAST_LEN = 4096

CHUNK = 64
Q_BLOCK = 128
ATTN_WIDTH = D_MODEL // 2
POOL_WIDTH = D_MODEL - ATTN_WIDTH
N_HEADS = 8
V_HEAD_DIM = ATTN_WIDTH // N_HEADS
QK_NOPE_DIM = 64
QK_ROPE_DIM = 32
Q_LORA_RANK = 256
KV_LORA_RANK = 128
ROPE_BASE = 10000.0
POOL_WINDOWS = (2, 4, 8, 16)
N_POOL_GROUPS = len(POOL_WINDOWS)
POOL_GROUP_DIM = POOL_WIDTH // N_POOL_GROUPS
POOL_STATE = max(POOL_WINDOWS) - 1
D_FF = -(-8 * D_MODEL // (3 * 256)) * 256
IN_WIDTH = Q_LORA_RANK + KV_LORA_RANK + QK_ROPE_DIM + POOL_WIDTH
SM_SCALE = 1.0 / math.sqrt(QK_NOPE_DIM + QK_ROPE_DIM)
EPS = 1e-6

kernel_name = "hymba_mla_pool_streaming_step"


def rmsnorm(x, g):
    xf = x.astype(jnp.float32)
    y = xf * lax.rsqrt(jnp.mean(xf * xf, axis=-1, keepdims=True) + EPS)
    return (y * g.astype(jnp.float32)).astype(x.dtype)


def rope(x, pos):
    d = x.shape[-1]
    freqs = jnp.power(ROPE_BASE, -jnp.arange(0, d, 2, dtype=jnp.float32) / d)
    ang = pos.astype(jnp.float32)[:, None] * freqs[None, :]
    cos = jnp.cos(ang)[None, :, None, :]
    sin = jnp.sin(ang)[None, :, None, :]
    xf = x.astype(jnp.float32)
    x1, x2 = xf[..., : d // 2], xf[..., d // 2:]
    return jnp.concatenate([x1 * cos - x2 * sin, x2 * cos + x1 * sin], axis=-1).astype(x.dtype)


def attend(q_lat, q_rope, keys_lat, keys_rope, q_pos, k_pos):
    s = (jnp.einsum('bqhr,bkr->bhqk', q_lat, keys_lat)
         + jnp.einsum('bqhd,bkd->bhqk', q_rope, keys_rope)).astype(jnp.float32) * SM_SCALE
    mask = (k_pos[None, :] // CHUNK) <= (q_pos[:, None] // CHUNK)
    s = jnp.where(mask[None, None], s, -jnp.inf)
    p = jax.nn.softmax(s, axis=-1).astype(keys_lat.dtype)
    return jnp.einsum('bhqk,bkr->bqhr', p, keys_lat)


def pool_mix(u, prev, w_pool, pool_scale, pos):
    B, T, _ = u.shape
    ext = jnp.concatenate([prev, u], axis=1)
    extf = ext.astype(jnp.float32)
    cs = jnp.concatenate([jnp.zeros((B, 1, POOL_WIDTH), jnp.float32),
                          jnp.cumsum(extf, axis=1)], axis=1)
    hi = cs[:, POOL_STATE + 1:]
    outs = []
    for g, w in enumerate(POOL_WINDOWS):
        sl = slice(g * POOL_GROUP_DIM, (g + 1) * POOL_GROUP_DIM)
        lo = cs[:, POOL_STATE + 1 - w: POOL_STATE + 1 - w + T, sl]
        cnt = jnp.minimum(pos + 1, w).astype(jnp.float32)[None, :, None]
        outs.append((hi[..., sl] - lo) / cnt - extf[:, POOL_STATE:, sl])
    pooled = jnp.stack(outs, axis=2).astype(u.dtype)
    mixed = jnp.einsum('btgc,gcd->btgd', pooled, w_pool).reshape(B, T, POOL_WIDTH)
    return mixed * pool_scale, ext[:, -POOL_STATE:]


def layer(x, pos, ckv_prev, kr_prev, pool_prev, g_norm1, w_in, g_q, w_uq, g_kv, w_uk, w_uv,
          w_pool, pool_scale, g_out_attn, g_out_pool, w_o, g_norm2, w_gate, w_up, w_down, blocked):
    B, T, _ = x.shape
    h = rmsnorm(x, g_norm1)
    z = h @ w_in
    a0 = Q_LORA_RANK
    a1 = a0 + KV_LORA_RANK
    a2 = a1 + QK_ROPE_DIM
    c_q, c_kv, k_r, u = z[..., :a0], z[..., a0:a1], z[..., a1:a2], z[..., a2:]
    q = (rmsnorm(c_q, g_q) @ w_uq).reshape(B, T, N_HEADS, QK_NOPE_DIM + QK_ROPE_DIM)
    q_nope, q_rope = q[..., :QK_NOPE_DIM], rope(q[..., QK_NOPE_DIM:], pos)
    q_lat = jnp.einsum('bthd,hrd->bthr', q_nope, w_uk)
    c_kv = rmsnorm(c_kv, g_kv)
    k_r = rope(k_r[:, :, None, :], pos)[:, :, 0, :]
    if ckv_prev is None:
        keys_lat, keys_rope, k_pos = c_kv, k_r, pos
    else:
        keys_lat = jnp.concatenate([ckv_prev, c_kv], axis=1)
        keys_rope = jnp.concatenate([kr_prev, k_r], axis=1)
        k_pos = jnp.concatenate([jnp.arange(ckv_prev.shape[1]), pos])
    if blocked:
        nb = T // Q_BLOCK
        ql_b = q_lat.reshape(B, nb, Q_BLOCK, N_HEADS, KV_LORA_RANK).transpose(1, 0, 2, 3, 4)
        qr_b = q_rope.reshape(B, nb, Q_BLOCK, N_HEADS, QK_ROPE_DIM).transpose(1, 0, 2, 3, 4)

        def blk(args):
            i, ql, qr = args
            qp = lax.dynamic_slice(pos, (i * Q_BLOCK,), (Q_BLOCK,))
            return attend(ql, qr, keys_lat, keys_rope, qp, k_pos)

        o_b = lax.map(blk, (jnp.arange(nb), ql_b, qr_b))
        o_lat = o_b.transpose(1, 0, 2, 3, 4).reshape(B, T, N_HEADS, KV_LORA_RANK)
    else:
        o_lat = attend(q_lat, q_rope, keys_lat, keys_rope, pos, k_pos)
    o_attn = jnp.einsum('bthr,hrd->bthd', o_lat, w_uv).reshape(B, T, ATTN_WIDTH)
    if pool_prev is None:
        pool_prev = jnp.zeros((B, POOL_STATE, POOL_WIDTH), u.dtype)
    o_pool, new_pool = pool_mix(u, pool_prev, w_pool, pool_scale, pos)
    mix = jnp.concatenate([rmsnorm(o_attn, g_out_attn), rmsnorm(o_pool, g_out_pool)], axis=-1) @ w_o
    x = x + mix
    h2 = rmsnorm(x, g_norm2)
    x = x + (jax.nn.silu(h2 @ w_gate) * (h2 @ w_up)) @ w_down
    return x, c_kv, k_r, new_pool


def setup_inputs(seed: int = 0) -> dict:
    key = jax.random.key(seed)
    ks = jax.random.split(key, 24)
    f32 = jnp.float32

    def nrm(k, shape, scale):
        return jax.random.normal(k, shape, f32) * scale

    def gain(k, shape):
        return 1.0 + 0.05 * jax.random.normal(k, shape, f32)

    L = DEPTH
    return {
        "x_prompt": nrm(ks[0], (BATCH, SEQ, D_MODEL), 1.0),
        "x_sample": nrm(ks[1], (DEC_BATCH, DEC_SEQ, D_MODEL), 1.0),
        "cache_kv_latent": nrm(ks[2], (L, DEC_BATCH, PAST_LEN, KV_LORA_RANK), 1.0),
        "cache_k_rope": nrm(ks[3], (L, DEC_BATCH, PAST_LEN, QK_ROPE_DIM), 1.0),
        "state_pool": nrm(ks[4], (L, DEC_BATCH, POOL_STATE, POOL_WIDTH), 1.0),
        "g_norm1": gain(ks[5], (L, D_MODEL)),
        "w_in": nrm(ks[6], (L, D_MODEL, IN_WIDTH), D_MODEL ** -0.5),
        "g_q": gain(ks[7], (L, Q_LORA_RANK)),
        "w_uq": nrm(ks[8], (L, Q_LORA_RANK, N_HEADS * (QK_NOPE_DIM + QK_ROPE_DIM)), Q_LORA_RANK ** -0.5),
        "g_kv": gain(ks[9], (L, KV_LORA_RANK)),
        "w_uk": nrm(ks[10], (L, N_HEADS, KV_LORA_RANK, QK_NOPE_DIM), KV_LORA_RANK ** -0.5),
        "w_uv": nrm(ks[11], (L, N_HEADS, KV_LORA_RANK, V_HEAD_DIM), KV_LORA_RANK ** -0.5),
        "w_pool": nrm(ks[12], (L, N_POOL_GROUPS, POOL_GROUP_DIM, POOL_GROUP_DIM), POOL_GROUP_DIM ** -0.5),
        "pool_scale": gain(ks[13], (L, POOL_WIDTH)),
        "g_out_attn": gain(ks[14], (L, ATTN_WIDTH)),
        "g_out_pool": gain(ks[15], (L, POOL_WIDTH)),
        "w_o": nrm(ks[16], (L, D_MODEL, D_MODEL), D_MODEL ** -0.5),
        "g_norm2": gain(ks[17], (L, D_MODEL)),
        "w_gate": nrm(ks[18], (L, D_MODEL, D_FF), D_MODEL ** -0.5),
        "w_up": nrm(ks[19], (L, D_MODEL, D_FF), D_MODEL ** -0.5),
        "w_down": nrm(ks[20], (L, D_FF, D_MODEL), D_FF ** -0.5),
        "g_final": gain(ks[21], (D_MODEL,)),
    }


def reference(x_prompt, x_sample, cache_kv_latent, cache_k_rope, state_pool,
              g_norm1, w_in, g_q, w_uq, g_kv, w_uk, w_uv, w_pool, pool_scale,
              g_out_attn, g_out_pool, w_o, g_norm2, w_gate, w_up, w_down, g_final):
    past = cache_kv_latent.shape[2]
    pos_p = jnp.arange(x_prompt.shape[1])
    pos_s = past + jnp.arange(x_sample.shape[1])
    xp, xs = x_prompt, x_sample
    ckv_p, kr_p, pool_p, ckv_s, kr_s, pool_s = [], [], [], [], [], []
    for l in range(DEPTH):
        w = (g_norm1[l], w_in[l], g_q[l], w_uq[l], g_kv[l], w_uk[l], w_uv[l], w_pool[l], pool_scale[l],
             g_out_attn[l], g_out_pool[l], w_o[l], g_norm2[l], w_gate[l], w_up[l], w_down[l])
        xp, a, b, c = layer(xp, pos_p, None, None, None, *w, blocked=True)
        ckv_p.append(a); kr_p.append(b); pool_p.append(c)
        xs, a, b, c = layer(xs, pos_s, cache_kv_latent[l], cache_k_rope[l], state_pool[l], *w, blocked=False)
        ckv_s.append(a); kr_s.append(b); pool_s.append(c)
    y_prompt = rmsnorm(xp, g_final)
    y_sample = rmsnorm(xs, g_final)
    return (y_prompt, y_sample,
            jnp.stack(ckv_p), jnp.stack(kr_p), jnp.stack(pool_p),
            jnp.stack(ckv_s), jnp.stack(kr_s), jnp.stack(pool_s))
```

```python
import functools
import math

import jax
import jax.numpy as jnp
from jax import lax
from jax.experimental import pallas as pl
from jax.experimental.pallas import tpu as pltpu

CHUNK = 64
N_HEADS = 8
QK_NOPE_DIM = 64
QK_ROPE_DIM = 32
Q_LORA_RANK = 256
KV_LORA_RANK = 128
ROPE_BASE = 10000.0
POOL_WINDOWS = (2, 4, 8, 16)
POOL_STATE = max(POOL_WINDOWS) - 1
POOL_PAD = POOL_STATE + 1
SM_SCALE = 1.0 / math.sqrt(QK_NOPE_DIM + QK_ROPE_DIM)
EPS = 1e-6
LANES = 128
QK_PAD = 2 * LANES
ROPE_ROLL = LANES - QK_ROPE_DIM
VMEM_LIMIT = 56 * 1024 * 1024

F32 = jnp.float32
BF16 = jnp.bfloat16


def _rms(x, g):
    ms = jnp.mean(x * x, axis=-1, keepdims=True)
    return x * lax.rsqrt(ms + EPS) * g


def _rope_block(blk, cos, sin):
    return blk * cos + pltpu.roll(blk, ROPE_ROLL, 1) * sin


def _const_spec(shape):
    nd = len(shape)
    return pl.BlockSpec(shape, lambda *_: (0,) * nd, pipeline_mode=pl.Buffered(1))


def _pre_kernel(x_ref, prev_ref, inv_ref, cos_ref, sin_ref, g1_ref, w1_ref, gq_ref, w2_ref, w3_ref,
                gkv_ref, wpool_ref, pscale_ref, gpool_ref,
                q_ref, k_ref, ckv_ref, kr_ref, opool_ref, newpool_ref, ext_ref, *, tm, pool_w):
    t = pl.program_id(1)
    n_t = pl.num_programs(1)
    x = x_ref[0]
    h = _rms(x, g1_ref[...]).astype(BF16)
    z = jnp.dot(h, w1_ref[...], preferred_element_type=F32)
    a0 = Q_LORA_RANK
    a1 = a0 + KV_LORA_RANK
    a2 = a1 + LANES
    cos = cos_ref[...]
    sin = sin_ref[...]

    ckv_n = _rms(z[:, a0:a1], gkv_ref[...])
    k_rot = _rope_block(z[:, a1:a2], cos, sin)
    ckv_ref[0] = ckv_n
    kr_ref[0] = k_rot[:, :QK_ROPE_DIM]
    k_ref[0, :, 0:LANES] = ckv_n.astype(BF16)
    k_ref[0, :, LANES:QK_PAD] = k_rot.astype(BF16)

    cqn = _rms(z[:, :a0], gq_ref[...]).astype(BF16)
    q = jnp.dot(cqn, w2_ref[...], preferred_element_type=F32)
    n_nope = N_HEADS * QK_NOPE_DIM
    q_lat = jnp.dot(q[:, :n_nope].astype(BF16), w3_ref[...], preferred_element_type=F32)
    for hh in range(N_HEADS):
        q_ref[0, hh, :, 0:LANES] = (q_lat[:, hh * LANES:(hh + 1) * LANES] * SM_SCALE).astype(BF16)
        blk = q[:, n_nope + hh * LANES:n_nope + (hh + 1) * LANES]
        q_ref[0, hh, :, LANES:QK_PAD] = (_rope_block(blk, cos, sin) * SM_SCALE).astype(BF16)

    u = z[:, a2:]

    @pl.when(t == 0)
    def _():
        ext_ref[0:POOL_PAD, :] = prev_ref[0]

    ext_ref[POOL_PAD:POOL_PAD + tm, :] = u
    mixed = []
    for g, w in enumerate(POOL_WINDOWS):
        lo, hi = g * LANES, (g + 1) * LANES
        s = ext_ref[:, lo:hi]
        k = 1
        while k < w:
            s = s + pltpu.roll(s, k, 0)
            k *= 2
        inv_head = jnp.where(t == 0, inv_ref[:, lo:hi], 1.0 / w)
        head = s[POOL_PAD:2 * POOL_PAD] * inv_head - u[:POOL_PAD, lo:hi]
        body = s[2 * POOL_PAD:] * (1.0 / w) - u[POOL_PAD:, lo:hi]
        pooled = jnp.concatenate([head, body], axis=0).astype(BF16)
        mixed.append(jnp.dot(pooled, wpool_ref[g], preferred_element_type=F32))
    o_pool = jnp.concatenate(mixed, axis=1) * pscale_ref[...]
    opool_ref[0] = _rms(o_pool, gpool_ref[...]).astype(BF16)

    @pl.when(t == n_t - 1)
    def _():
        newpool_ref[0] = ext_ref[tm + 1:tm + POOL_PAD, :]

    ext_ref[0:POOL_PAD, :] = ext_ref[tm:tm + POOL_PAD, :]


def _pre_call(x, prev, inv_head, cos, sin, wts, *, tm):
    B, T, D = x.shape
    pool_w = prev.shape[-1]
    assert T % tm == 0 and T >= POOL_PAD and tm >= 2 * POOL_PAD and pool_w == len(POOL_WINDOWS) * LANES
    grid = (B, T // tm)
    in_specs = [
        pl.BlockSpec((1, tm, D), lambda b, t: (b, t, 0)),
        pl.BlockSpec((1, POOL_PAD, pool_w), lambda b, t: (b, 0, 0)),
        _const_spec(inv_head.shape),
        pl.BlockSpec((tm, LANES), lambda b, t: (t, 0)),
        pl.BlockSpec((tm, LANES), lambda b, t: (t, 0)),
    ] + [_const_spec(w.shape) for w in wts]
    out_shape = (
        jax.ShapeDtypeStruct((B, N_HEADS, T, QK_PAD), BF16),
        jax.ShapeDtypeStruct((B, T, QK_PAD), BF16),
        jax.ShapeDtypeStruct((B, T, KV_LORA_RANK), F32),
        jax.ShapeDtypeStruct((B, T, QK_ROPE_DIM), F32),
        jax.ShapeDtypeStruct((B, T, pool_w), BF16),
        jax.ShapeDtypeStruct((B, POOL_STATE, pool_w), F32),
    )
    out_specs = (
        pl.BlockSpec((1, N_HEADS, tm, QK_PAD), lambda b, t: (b, 0, t, 0)),
        pl.BlockSpec((1, tm, QK_PAD), lambda b, t: (b, t, 0)),
        pl.BlockSpec((1, tm, KV_LORA_RANK), lambda b, t: (b, t, 0)),
        pl.BlockSpec((1, tm, QK_ROPE_DIM), lambda b, t: (b, t, 0)),
        pl.BlockSpec((1, tm, pool_w), lambda b, t: (b, t, 0)),
        pl.BlockSpec((1, POOL_STATE, pool_w), lambda b, t: (b, 0, 0)),
    )
    return pl.pallas_call(
        functools.partial(_pre_kernel, tm=tm, pool_w=pool_w),
        grid=grid, in_specs=in_specs, out_specs=out_specs, out_shape=out_shape,
        scratch_shapes=[pltpu.VMEM((POOL_PAD + tm, pool_w), F32)],
        compiler_params=pltpu.CompilerParams(
            dimension_semantics=("arbitrary", "arbitrary"), vmem_limit_bytes=VMEM_LIMIT),
        name=f"pre_t{tm}",
    )(x, prev, inv_head, cos, sin, *wts)


def _softmax_step(s, v, m_ref, l_ref, acc_ref, first):
    m_cur = jnp.max(s, axis=-1, keepdims=True)
    if first:
        m_new = m_cur
    else:
        m_prev = m_ref[...]
        m_new = jnp.maximum(m_prev, m_cur)
        alpha = jnp.exp(m_prev - m_new)
    p = jnp.exp(s - m_new)
    l_cur = jnp.sum(p, axis=-1, keepdims=True)
    pv = jnp.dot(p.astype(BF16), v, preferred_element_type=F32)
    if first:
        l_ref[...] = l_cur
        acc_ref[...] = pv
    else:
        l_ref[...] = alpha * l_ref[...] + l_cur
        acc_ref[...] = alpha * acc_ref[...] + pv
    m_ref[...] = m_new


def _attn_prompt_kernel(q_ref, k_ref, o_ref, m_ref, l_ref, acc_ref, *, tq):
    qi = pl.program_id(1)
    rows = N_HEADS * tq
    q = q_ref[0].reshape(rows, QK_PAD)
    nt = (((1,), (1,)), ((), ()))

    k_diag = k_ref[0, pl.ds(pl.multiple_of(qi * tq, tq), tq), :]
    s = lax.dot_general(q, k_diag, nt, preferred_element_type=F32)
    q_chunk = (lax.broadcasted_iota(jnp.int32, (rows, tq), 0) % tq) // CHUNK
    k_chunk = lax.broadcasted_iota(jnp.int32, (rows, tq), 1) // CHUNK
    s = jnp.where(k_chunk <= q_chunk, s, -jnp.inf)
    _softmax_step(s, k_diag[:, :KV_LORA_RANK], m_ref, l_ref, acc_ref, first=True)

    def body(j, carry):
        k_blk = k_ref[0, pl.ds(pl.multiple_of(j * tq, tq), tq), :]
        s = lax.dot_general(q, k_blk, nt, preferred_element_type=F32)
        _softmax_step(s, k_blk[:, :KV_LORA_RANK], m_ref, l_ref, acc_ref, first=False)
        return carry

    lax.fori_loop(0, qi, body, 0)

    o = (acc_ref[...] / l_ref[...]).astype(BF16)
    for hh in range(N_HEADS):
        o_ref[0, :, hh * LANES:(hh + 1) * LANES] = o[hh * tq:(hh + 1) * tq]


def _attn_prompt_call(q, k, *, tq):
    B, H, T, _ = q.shape
    assert T % tq == 0 and tq % CHUNK == 0
    rows = H * tq
    return pl.pallas_call(
        functools.partial(_attn_prompt_kernel, tq=tq),
        grid=(B, T // tq),
        in_specs=[
            pl.BlockSpec((1, H, tq, QK_PAD), lambda b, i: (b, 0, i, 0)),
            pl.BlockSpec((1, T, QK_PAD), lambda b, i: (b, 0, 0)),
        ],
        out_specs=pl.BlockSpec((1, tq, H * KV_LORA_RANK), lambda b, i: (b, i, 0)),
        out_shape=jax.ShapeDtypeStruct((B, T, H * KV_LORA_RANK), BF16),
        scratch_shapes=[pltpu.VMEM((rows, 1), F32), pltpu.VMEM((rows, 1), F32),
                        pltpu.VMEM((rows, KV_LORA_RANK), F32)],
        compiler_params=pltpu.CompilerParams(
            dimension_semantics=("arbitrary", "arbitrary"), vmem_limit_bytes=VMEM_LIMIT),
        name="attn_prompt",
    )(q, k)


def _attn_sample_kernel(q_ref, ckv_ref, ckr_ref, knew_ref, o_ref, *, tq, past):
    rows = N_HEADS * tq
    q = q_ref[0].reshape(rows, QK_PAD)
    q_lat = q[:, :KV_LORA_RANK]
    q_rope = q[:, KV_LORA_RANK:KV_LORA_RANK + QK_ROPE_DIM]
    nt = (((1,), (1,)), ((), ()))
    v_past = ckv_ref[0].astype(BF16)
    s_past = (lax.dot_general(q_lat, v_past, nt, preferred_element_type=F32)
              + lax.dot_general(q_rope, ckr_ref[0].astype(BF16), nt, preferred_element_type=F32))
    k_new = knew_ref[0]
    s_new = lax.dot_general(q, k_new, nt, preferred_element_type=F32)

    def masked(s, k_start):
        q_pos = past + lax.broadcasted_iota(jnp.int32, s.shape, 0) % tq
        k_pos = k_start + lax.broadcasted_iota(jnp.int32, s.shape, 1)
        return jnp.where(k_pos // CHUNK <= q_pos // CHUNK, s, -jnp.inf)

    s_past = masked(s_past, 0)
    s_new = masked(s_new, past)
    m = jnp.maximum(jnp.max(s_past, axis=-1, keepdims=True), jnp.max(s_new, axis=-1, keepdims=True))
    p_past = jnp.exp(s_past - m)
    p_new = jnp.exp(s_new - m)
    l = jnp.sum(p_past, axis=-1, keepdims=True) + jnp.sum(p_new, axis=-1, keepdims=True)
    acc = (jnp.dot(p_past.astype(BF16), v_past, preferred_element_type=F32)
           + jnp.dot(p_new.astype(BF16), k_new[:, :KV_LORA_RANK], preferred_element_type=F32))
    o = (acc / l).astype(BF16)
    for hh in range(N_HEADS):
        o_ref[0, :, hh * LANES:(hh + 1) * LANES] = o[hh * tq:(hh + 1) * tq]


def _attn_sample_call(q, cache_kv, cache_kr, k_new):
    B, H, tq, _ = q.shape
    past = cache_kv.shape[1]
    return pl.pallas_call(
        functools.partial(_attn_sample_kernel, tq=tq, past=past),
        grid=(B,),
        in_specs=[
            pl.BlockSpec((1, H, tq, QK_PAD), lambda b: (b, 0, 0, 0)),
            pl.BlockSpec((1, past, KV_LORA_RANK), lambda b: (b, 0, 0)),
            pl.BlockSpec((1, past, QK_ROPE_DIM), lambda b: (b, 0, 0)),
            pl.BlockSpec((1, tq, QK_PAD), lambda b: (b, 0, 0)),
        ],
        out_specs=pl.BlockSpec((1, tq, H * KV_LORA_RANK), lambda b: (b, 0, 0)),
        out_shape=jax.ShapeDtypeStruct((B, tq, H * KV_LORA_RANK), BF16),
        compiler_params=pltpu.CompilerParams(
            dimension_semantics=("arbitrary",), vmem_limit_bytes=VMEM_LIMIT),
        name="attn_sample",
    )(q, cache_kv, cache_kr, k_new)


def _post_kernel(x_ref, olat_ref, opool_ref, wuv_ref, gattn_ref, wo_ref, g2_ref, wg_ref, wu_ref, wd_ref,
                 gf_ref, y_ref):
    o_attn = jnp.dot(olat_ref[0], wuv_ref[...], preferred_element_type=F32)
    a_n = _rms(o_attn, gattn_ref[...]).astype(BF16)
    merged = jnp.concatenate([a_n, opool_ref[0]], axis=1)
    x1 = x_ref[0] + jnp.dot(merged, wo_ref[...], preferred_element_type=F32)
    h2 = _rms(x1, g2_ref[...]).astype(BF16)
    gate = jnp.dot(h2, wg_ref[...], preferred_element_type=F32)
    up = jnp.dot(h2, wu_ref[...], preferred_element_type=F32)
    act = (gate * jax.nn.sigmoid(gate) * up).astype(BF16)
    x2 = x1 + jnp.dot(act, wd_ref[...], preferred_element_type=F32)
    y_ref[0] = _rms(x2, gf_ref[...])


def _post_call(x, o_lat, o_pool, wts, *, tm):
    B, T, D = x.shape
    assert T % tm == 0
    in_specs = [
        pl.BlockSpec((1, tm, D), lambda b, t: (b, t, 0)),
        pl.BlockSpec((1, tm, o_lat.shape[-1]), lambda b, t: (b, t, 0)),
        pl.BlockSpec((1, tm, o_pool.shape[-1]), lambda b, t: (b, t, 0)),
    ] + [_const_spec(w.shape) for w in wts]
    return pl.pallas_call(
        _post_kernel,
        grid=(B, T // tm),
        in_specs=in_specs,
        out_specs=pl.BlockSpec((1, tm, D), lambda b, t: (b, t, 0)),
        out_shape=jax.ShapeDtypeStruct((B, T, D), F32),
        compiler_params=pltpu.CompilerParams(
            dimension_semantics=("arbitrary", "arbitrary"), vmem_limit_bytes=VMEM_LIMIT),
        name=f"post_t{tm}",
    )(x, o_lat, o_pool, *wts)


def _swap_halves(w):
    half = w.shape[-1] // 2
    return jnp.concatenate([-w[..., half:], w[..., :half]], axis=-1)


def _rope_cols(w):
    pad = jnp.zeros(w.shape[:-1] + (LANES - 2 * QK_ROPE_DIM,), w.dtype)
    return jnp.concatenate([w, _swap_halves(w), pad], axis=-1)


def _rope_tables(pos):
    d = QK_ROPE_DIM
    freqs = jnp.power(ROPE_BASE, -jnp.arange(0, d, 2, dtype=F32) / d)
    ang = pos.astype(F32)[:, None] * freqs[None, :]
    pad = jnp.zeros((pos.shape[0], LANES - d), F32)
    cos = jnp.concatenate([jnp.cos(ang), jnp.cos(ang), pad], axis=-1)
    sin = jnp.concatenate([jnp.sin(ang), jnp.sin(ang), pad], axis=-1)
    return cos, sin


def _pool_inv_head(pos0, pool_w):
    pos = pos0 + jnp.arange(POOL_PAD)
    win = jnp.repeat(jnp.asarray(POOL_WINDOWS, jnp.int32), pool_w // len(POOL_WINDOWS))
    return 1.0 / jnp.minimum(pos[:, None] + 1, win[None, :]).astype(F32)


def _layer(x, pos0, prev, attn_fn, pre_w, post_w, *, tm_pre, tm_post):
    T = x.shape[1]
    cos, sin = _rope_tables(pos0 + jnp.arange(T))
    inv_head = _pool_inv_head(pos0, prev.shape[-1])
    q, k, ckv, kr, o_pool, new_pool = _pre_call(x, prev, inv_head, cos, sin, pre_w, tm=tm_pre)
    o_lat = attn_fn(q, k)
    y = _post_call(x, o_lat, o_pool, post_w, tm=tm_post)
    return y, ckv, kr, new_pool


def kernel(x_prompt, x_sample, cache_kv_latent, cache_k_rope, state_pool, g_norm1, w_in, g_q, w_uq, g_kv,
           w_uk, w_uv, w_pool, pool_scale, g_out_attn, g_out_pool, w_o, g_norm2, w_gate, w_up, w_down,
           g_final):
    depth = w_in.shape[0]
    assert depth == 1
    l = 0
    pool_w = w_pool.shape[1] * w_pool.shape[2]
    a0 = Q_LORA_RANK
    a1 = a0 + KV_LORA_RANK
    a2 = a1 + QK_ROPE_DIM
    win = w_in[l]
    w1 = jnp.concatenate([win[:, :a1], _rope_cols(win[:, a1:a2]), win[:, a2:]], axis=1).astype(BF16)
    wq = w_uq[l].reshape(Q_LORA_RANK, N_HEADS, QK_NOPE_DIM + QK_ROPE_DIM)
    w2 = jnp.concatenate([wq[..., :QK_NOPE_DIM].reshape(Q_LORA_RANK, -1),
                          _rope_cols(wq[..., QK_NOPE_DIM:]).reshape(Q_LORA_RANK, -1)], axis=1).astype(BF16)
    eye = jnp.eye(N_HEADS, dtype=F32)
    w3 = jnp.einsum('hrd,hg->hdgr', w_uk[l], eye).reshape(N_HEADS * QK_NOPE_DIM, N_HEADS * KV_LORA_RANK)
    wuv = jnp.einsum('hrd,hg->hrgd', w_uv[l], eye).reshape(N_HEADS * KV_LORA_RANK, -1)
    row = lambda g: g.reshape(1, -1).astype(F32)
    pre_w = (row(g_norm1[l]), w1, row(g_q[l]), w2, w3.astype(BF16), row(g_kv[l]), w_pool[l].astype(BF16),
             row(pool_scale[l]), row(g_out_pool[l]))
    post_w = (wuv.astype(BF16), row(g_out_attn[l]), w_o[l].astype(BF16), row(g_norm2[l]),
              w_gate[l].astype(BF16), w_up[l].astype(BF16), w_down[l].astype(BF16), row(g_final))

    B, T, _ = x_prompt.shape
    prev_p = jnp.zeros((B, POOL_PAD, pool_w), F32)
    y_p, ckv_p, kr_p, pool_p = _layer(
        x_prompt, 0, prev_p, functools.partial(_attn_prompt_call, tq=256), pre_w, post_w,
        tm_pre=512, tm_post=256)

    past = cache_kv_latent.shape[2]
    Ts = x_sample.shape[1]
    prev_s = jnp.pad(state_pool[l], ((0, 0), (POOL_PAD - POOL_STATE, 0), (0, 0)))
    attn_s = lambda q, k: _attn_sample_call(q, cache_kv_latent[l], cache_k_rope[l], k)
    y_s, ckv_s, kr_s, pool_s = _layer(x_sample, past, prev_s, attn_s, pre_w, post_w, tm_pre=Ts, tm_post=Ts)

    return (y_p, y_s, ckv_p[None], kr_p[None], pool_p[None], ckv_s[None], kr_s[None], pool_s[None])
```

```python
import functools
import math

import jax
import jax.numpy as jnp
from jax import lax
from jax.experimental import pallas as pl
from jax.experimental.pallas import tpu as pltpu

CHUNK = 64
N_HEADS = 8
QK_NOPE_DIM = 64
QK_ROPE_DIM = 32
Q_LORA_RANK = 256
KV_LORA_RANK = 128
ROPE_BASE = 10000.0
POOL_WINDOWS = (2, 4, 8, 16)
POOL_STATE = max(POOL_WINDOWS) - 1
POOL_PAD = POOL_STATE + 1
SM_SCALE = 1.0 / math.sqrt(QK_NOPE_DIM + QK_ROPE_DIM)
Q_SCALE = SM_SCALE * math.log2(math.e)
EPS = 1e-6
ATTN_SLAB = 128
LANES = 128
QK_PAD = 2 * LANES
ROPE_ROLL = LANES - QK_ROPE_DIM
VMEM_LIMIT = 56 * 1024 * 1024

F32 = jnp.float32
BF16 = jnp.bfloat16


def _rms(x, g):
    ms = jnp.mean(x * x, axis=-1, keepdims=True)
    return x * lax.rsqrt(ms + EPS) * g


def _rope_block(blk, cos, sin):
    return blk * cos + pltpu.roll(blk, ROPE_ROLL, 1) * sin


def _const_spec(shape):
    nd = len(shape)
    return pl.BlockSpec(shape, lambda *_: (0,) * nd, pipeline_mode=pl.Buffered(1))


def _pre_kernel(x_ref, prev_ref, inv_ref, cos_ref, sin_ref, g1_ref, w1_ref, gq_ref, w2_ref, w3_ref,
                gkv_ref, wpool_ref, pscale_ref, gpool_ref,
                q_ref, k_ref, ckv_ref, kr_ref, opool_ref, newpool_ref, ext_ref, *, tm, pool_w):
    t = pl.program_id(1)
    n_t = pl.num_programs(1)
    x = x_ref[0]
    h = _rms(x, g1_ref[...]).astype(BF16)
    z = jnp.dot(h, w1_ref[...], preferred_element_type=F32)
    a0 = Q_LORA_RANK
    a1 = a0 + KV_LORA_RANK
    a2 = a1 + LANES
    cos = cos_ref[...]
    sin = sin_ref[...]

    ckv_n = _rms(z[:, a0:a1], gkv_ref[...])
    k_rot = _rope_block(z[:, a1:a2], cos, sin)
    ckv_ref[0] = ckv_n
    kr_ref[0] = k_rot[:, :QK_ROPE_DIM]
    k_ref[0, :, 0:LANES] = ckv_n.astype(BF16)
    k_ref[0, :, LANES:QK_PAD] = k_rot.astype(BF16)

    cqn = _rms(z[:, :a0], gq_ref[...]).astype(BF16)
    q = jnp.dot(cqn, w2_ref[...], preferred_element_type=F32)
    n_nope = N_HEADS * QK_NOPE_DIM
    q_lat = jnp.dot(q[:, :n_nope].astype(BF16), w3_ref[...], preferred_element_type=F32)
    for hh in range(N_HEADS):
        q_ref[0, hh, :, 0:LANES] = (q_lat[:, hh * LANES:(hh + 1) * LANES] * Q_SCALE).astype(BF16)
        blk = q[:, n_nope + hh * LANES:n_nope + (hh + 1) * LANES]
        q_ref[0, hh, :, LANES:QK_PAD] = (_rope_block(blk, cos, sin) * Q_SCALE).astype(BF16)

    u = z[:, a2:]

    @pl.when(t == 0)
    def _():
        ext_ref[0:POOL_PAD, :] = prev_ref[0]

    ext_ref[POOL_PAD:POOL_PAD + tm, :] = u
    mixed = []
    for g, w in enumerate(POOL_WINDOWS):
        lo, hi = g * LANES, (g + 1) * LANES
        s = ext_ref[:, lo:hi]
        k = 1
        while k < w:
            s = s + pltpu.roll(s, k, 0)
            k *= 2
        inv_head = jnp.where(t == 0, inv_ref[:, lo:hi], 1.0 / w)
        head = s[POOL_PAD:2 * POOL_PAD] * inv_head - u[:POOL_PAD, lo:hi]
        body = s[2 * POOL_PAD:] * (1.0 / w) - u[POOL_PAD:, lo:hi]
        pooled = jnp.concatenate([head, body], axis=0).astype(BF16)
        mixed.append(jnp.dot(pooled, wpool_ref[g], preferred_element_type=F32))
    o_pool = jnp.concatenate(mixed, axis=1) * pscale_ref[...]
    opool_ref[0] = _rms(o_pool, gpool_ref[...]).astype(BF16)

    @pl.when(t == n_t - 1)
    def _():
        newpool_ref[0] = ext_ref[tm + 1:tm + POOL_PAD, :]

    ext_ref[0:POOL_PAD, :] = ext_ref[tm:tm + POOL_PAD, :]


def _pre_call(x, prev, inv_head, cos, sin, wts, *, tm):
    B, T, D = x.shape
    pool_w = prev.shape[-1]
    assert T % tm == 0 and T >= POOL_PAD and tm >= 2 * POOL_PAD and pool_w == len(POOL_WINDOWS) * LANES
    grid = (B, T // tm)
    in_specs = [
        pl.BlockSpec((1, tm, D), lambda b, t: (b, t, 0)),
        pl.BlockSpec((1, POOL_PAD, pool_w), lambda b, t: (b, 0, 0)),
        _const_spec(inv_head.shape),
        pl.BlockSpec((tm, LANES), lambda b, t: (t, 0)),
        pl.BlockSpec((tm, LANES), lambda b, t: (t, 0)),
    ] + [_const_spec(w.shape) for w in wts]
    out_shape = (
        jax.ShapeDtypeStruct((B, N_HEADS, T, QK_PAD), BF16),
        jax.ShapeDtypeStruct((B, T, QK_PAD), BF16),
        jax.ShapeDtypeStruct((B, T, KV_LORA_RANK), F32),
        jax.ShapeDtypeStruct((B, T, QK_ROPE_DIM), F32),
        jax.ShapeDtypeStruct((B, T, pool_w), BF16),
        jax.ShapeDtypeStruct((B, POOL_STATE, pool_w), F32),
    )
    out_specs = (
        pl.BlockSpec((1, N_HEADS, tm, QK_PAD), lambda b, t: (b, 0, t, 0)),
        pl.BlockSpec((1, tm, QK_PAD), lambda b, t: (b, t, 0)),
        pl.BlockSpec((1, tm, KV_LORA_RANK), lambda b, t: (b, t, 0)),
        pl.BlockSpec((1, tm, QK_ROPE_DIM), lambda b, t: (b, t, 0)),
        pl.BlockSpec((1, tm, pool_w), lambda b, t: (b, t, 0)),
        pl.BlockSpec((1, POOL_STATE, pool_w), lambda b, t: (b, 0, 0)),
    )
    return pl.pallas_call(
        functools.partial(_pre_kernel, tm=tm, pool_w=pool_w),
        grid=grid, in_specs=in_specs, out_specs=out_specs, out_shape=out_shape,
        scratch_shapes=[pltpu.VMEM((POOL_PAD + tm, pool_w), F32)],
        compiler_params=pltpu.CompilerParams(
            dimension_semantics=("arbitrary", "arbitrary"), vmem_limit_bytes=VMEM_LIMIT),
        name=f"pre_t{tm}",
    )(x, prev, inv_head, cos, sin, *wts)


def _attn_prompt_kernel(q_ref, k_ref, o_ref, s0_ref, s1_ref, p_ref, alpha_ref, m_ref, l_ref, acc_ref, *, tq, tk):
    qi = pl.program_id(1)
    rows = N_HEADS * tq
    nt = (((1,), (1,)), ((), ()))
    n_full = (qi * tq) // tk

    def scores(j, s_ref):
        k_blk = k_ref[0, pl.ds(pl.multiple_of(j * tk, tk), tk), :]
        s_ref[...] = lax.dot_general(q_ref[0].reshape(rows, QK_PAD), k_blk, nt, preferred_element_type=F32)

    def lane_parts(s):
        return [s[:, c * LANES:(c + 1) * LANES] for c in range(tk // LANES)]

    def softmax_pv(j, s_ref, masked):
        for i in range(rows // ATTN_SLAB):
            r = slice(i * ATTN_SLAB, (i + 1) * ATTN_SLAB)
            s = s_ref[r, :]
            if masked:
                tok = qi * tq + (i * ATTN_SLAB) % tq + lax.broadcasted_iota(jnp.int32, s.shape, 0)
                key = j * tk + lax.broadcasted_iota(jnp.int32, s.shape, 1)
                s = jnp.where(key // CHUNK <= tok // CHUNK, s, -jnp.inf)
            parts = lane_parts(s)
            m_prev = m_ref[r, :]
            m_new = jnp.maximum(m_prev, jnp.max(functools.reduce(jnp.maximum, parts), axis=-1, keepdims=True))
            alpha = jnp.exp2(m_prev - m_new)
            probs = [jnp.exp2(part - m_new) for part in parts]
            l_ref[r, :] = alpha * l_ref[r, :] + functools.reduce(jnp.add, probs)
            m_ref[r, :] = m_new
            alpha_ref[r, :] = alpha
            for c, prob in enumerate(probs):
                p_ref[r, c * LANES:(c + 1) * LANES] = prob.astype(BF16)
        v = k_ref[0, pl.ds(pl.multiple_of(j * tk, tk), tk), 0:KV_LORA_RANK]
        acc_ref[...] = alpha_ref[...] * acc_ref[...] + jnp.dot(p_ref[...], v, preferred_element_type=F32)

    m_ref[...] = jnp.full(m_ref.shape, -jnp.inf, F32)
    l_ref[...] = jnp.zeros(l_ref.shape, F32)
    acc_ref[...] = jnp.zeros(acc_ref.shape, F32)

    scores(0, s0_ref)

    def pair(i, carry):
        j = 2 * i
        scores(j + 1, s1_ref)
        softmax_pv(j, s0_ref, False)
        scores(j + 2, s0_ref)
        softmax_pv(j + 1, s1_ref, False)
        return carry

    lax.fori_loop(0, n_full // 2, pair, 0)

    @pl.when(n_full % 2 == 0)
    def _():
        softmax_pv(n_full, s0_ref, True)

    @pl.when(n_full % 2 == 1)
    def _():
        scores(n_full, s1_ref)
        softmax_pv(n_full - 1, s0_ref, False)
        softmax_pv(n_full, s1_ref, True)

    l = jnp.sum(l_ref[...], axis=-1, keepdims=True)
    o = (acc_ref[...] / l).astype(BF16)
    for hh in range(N_HEADS):
        o_ref[0, :, hh * LANES:(hh + 1) * LANES] = o[hh * tq:(hh + 1) * tq]


def _attn_prompt_call(q, k, *, tq, tk):
    B, H, T, _ = q.shape
    assert T % tq == 0 and T % tk == 0 and tq % CHUNK == 0 and tq % ATTN_SLAB == 0 and tk % LANES == 0
    rows = H * tq
    return pl.pallas_call(
        functools.partial(_attn_prompt_kernel, tq=tq, tk=tk),
        grid=(B, T // tq),
        in_specs=[
            pl.BlockSpec((1, H, tq, QK_PAD), lambda b, i: (b, 0, i, 0)),
            pl.BlockSpec((1, T, QK_PAD), lambda b, i: (b, 0, 0)),
        ],
        out_specs=pl.BlockSpec((1, tq, H * KV_LORA_RANK), lambda b, i: (b, i, 0)),
        out_shape=jax.ShapeDtypeStruct((B, T, H * KV_LORA_RANK), BF16),
        scratch_shapes=[pltpu.VMEM((rows, tk), F32), pltpu.VMEM((rows, tk), F32),
                        pltpu.VMEM((rows, tk), BF16), pltpu.VMEM((rows, LANES), F32),
                        pltpu.VMEM((rows, LANES), F32), pltpu.VMEM((rows, LANES), F32),
                        pltpu.VMEM((rows, KV_LORA_RANK), F32)],
        compiler_params=pltpu.CompilerParams(
            dimension_semantics=("arbitrary", "arbitrary"), vmem_limit_bytes=VMEM_LIMIT),
        name="attn_prompt",
    )(q, k)


def _attn_sample_kernel(q_ref, ckv_ref, ckr_ref, knew_ref, o_ref, *, tq, past):
    rows = N_HEADS * tq
    q = q_ref[0].reshape(rows, QK_PAD)
    q_lat = q[:, :KV_LORA_RANK]
    q_rope = q[:, KV_LORA_RANK:KV_LORA_RANK + QK_ROPE_DIM]
    nt = (((1,), (1,)), ((), ()))
    v_past = ckv_ref[0].astype(BF16)
    s_past = (lax.dot_general(q_lat, v_past, nt, preferred_element_type=F32)
              + lax.dot_general(q_rope, ckr_ref[0].astype(BF16), nt, preferred_element_type=F32))
    k_new = knew_ref[0]
    s_new = lax.dot_general(q, k_new, nt, preferred_element_type=F32)

    def masked(s, k_start):
        q_pos = past + lax.broadcasted_iota(jnp.int32, s.shape, 0) % tq
        k_pos = k_start + lax.broadcasted_iota(jnp.int32, s.shape, 1)
        return jnp.where(k_pos // CHUNK <= q_pos // CHUNK, s, -jnp.inf)

    s_past = masked(s_past, 0)
    s_new = masked(s_new, past)
    m = jnp.maximum(jnp.max(s_past, axis=-1, keepdims=True), jnp.max(s_new, axis=-1, keepdims=True))
    p_past = jnp.exp2(s_past - m)
    p_new = jnp.exp2(s_new - m)
    l = jnp.sum(p_past, axis=-1, keepdims=True) + jnp.sum(p_new, axis=-1, keepdims=True)
    acc = (jnp.dot(p_past.astype(BF16), v_past, preferred_element_type=F32)
           + jnp.dot(p_new.astype(BF16), k_new[:, :KV_LORA_RANK], preferred_element_type=F32))
    o = (acc / l).astype(BF16)
    for hh in range(N_HEADS):
        o_ref[0, :, hh * LANES:(hh + 1) * LANES] = o[hh * tq:(hh + 1) * tq]


def _attn_sample_call(q, cache_kv, cache_kr, k_new):
    B, H, tq, _ = q.shape
    past = cache_kv.shape[1]
    return pl.pallas_call(
        functools.partial(_attn_sample_kernel, tq=tq, past=past),
        grid=(B,),
        in_specs=[
            pl.BlockSpec((1, H, tq, QK_PAD), lambda b: (b, 0, 0, 0)),
            pl.BlockSpec((1, past, KV_LORA_RANK), lambda b: (b, 0, 0)),
            pl.BlockSpec((1, past, QK_ROPE_DIM), lambda b: (b, 0, 0)),
            pl.BlockSpec((1, tq, QK_PAD), lambda b: (b, 0, 0)),
        ],
        out_specs=pl.BlockSpec((1, tq, H * KV_LORA_RANK), lambda b: (b, 0, 0)),
        out_shape=jax.ShapeDtypeStruct((B, tq, H * KV_LORA_RANK), BF16),
        compiler_params=pltpu.CompilerParams(
            dimension_semantics=("arbitrary",), vmem_limit_bytes=VMEM_LIMIT),
        name="attn_sample",
    )(q, cache_kv, cache_kr, k_new)


def _post_kernel(x_ref, olat_ref, opool_ref, wuv_ref, gattn_ref, wo_ref, g2_ref, wg_ref, wu_ref, wd_ref,
                 gf_ref, y_ref):
    o_attn = jnp.dot(olat_ref[0], wuv_ref[...], preferred_element_type=F32)
    a_n = _rms(o_attn, gattn_ref[...]).astype(BF16)
    merged = jnp.concatenate([a_n, opool_ref[0]], axis=1)
    x1 = x_ref[0] + jnp.dot(merged, wo_ref[...], preferred_element_type=F32)
    h2 = _rms(x1, g2_ref[...]).astype(BF16)
    gate = jnp.dot(h2, wg_ref[...], preferred_element_type=F32)
    up = jnp.dot(h2, wu_ref[...], preferred_element_type=F32)
    act = (gate * jax.nn.sigmoid(gate) * up).astype(BF16)
    x2 = x1 + jnp.dot(act, wd_ref[...], preferred_element_type=F32)
    y_ref[0] = _rms(x2, gf_ref[...])


def _post_call(x, o_lat, o_pool, wts, *, tm):
    B, T, D = x.shape
    assert T % tm == 0
    in_specs = [
        pl.BlockSpec((1, tm, D), lambda b, t: (b, t, 0)),
        pl.BlockSpec((1, tm, o_lat.shape[-1]), lambda b, t: (b, t, 0)),
        pl.BlockSpec((1, tm, o_pool.shape[-1]), lambda b, t: (b, t, 0)),
    ] + [_const_spec(w.shape) for w in wts]
    return pl.pallas_call(
        _post_kernel,
        grid=(B, T // tm),
        in_specs=in_specs,
        out_specs=pl.BlockSpec((1, tm, D), lambda b, t: (b, t, 0)),
        out_shape=jax.ShapeDtypeStruct((B, T, D), F32),
        compiler_params=pltpu.CompilerParams(
            dimension_semantics=("arbitrary", "arbitrary"), vmem_limit_bytes=VMEM_LIMIT),
        name=f"post_t{tm}",
    )(x, o_lat, o_pool, *wts)


def _swap_halves(w):
    half = w.shape[-1] // 2
    return jnp.concatenate([-w[..., half:], w[..., :half]], axis=-1)


def _rope_cols(w):
    pad = jnp.zeros(w.shape[:-1] + (LANES - 2 * QK_ROPE_DIM,), w.dtype)
    return jnp.concatenate([w, _swap_halves(w), pad], axis=-1)


def _rope_tables(pos):
    d = QK_ROPE_DIM
    freqs = jnp.power(ROPE_BASE, -jnp.arange(0, d, 2, dtype=F32) / d)
    ang = pos.astype(F32)[:, None] * freqs[None, :]
    pad = jnp.zeros((pos.shape[0], LANES - d), F32)
    cos = jnp.concatenate([jnp.cos(ang), jnp.cos(ang), pad], axis=-1)
    sin = jnp.concatenate([jnp.sin(ang), jnp.sin(ang), pad], axis=-1)
    return cos, sin


def _pool_inv_head(pos0, pool_w):
    pos = pos0 + jnp.arange(POOL_PAD)
    win = jnp.repeat(jnp.asarray(POOL_WINDOWS, jnp.int32), pool_w // len(POOL_WINDOWS))
    return 1.0 / jnp.minimum(pos[:, None] + 1, win[None, :]).astype(F32)


def _layer(x, pos0, prev, attn_fn, pre_w, post_w, *, tm_pre, tm_post):
    T = x.shape[1]
    cos, sin = _rope_tables(pos0 + jnp.arange(T))
    inv_head = _pool_inv_head(pos0, prev.shape[-1])
    q, k, ckv, kr, o_pool, new_pool = _pre_call(x, prev, inv_head, cos, sin, pre_w, tm=tm_pre)
    o_lat = attn_fn(q, k)
    y = _post_call(x, o_lat, o_pool, post_w, tm=tm_post)
    return y, ckv, kr, new_pool


def kernel(x_prompt, x_sample, cache_kv_latent, cache_k_rope, state_pool, g_norm1, w_in, g_q, w_uq, g_kv,
           w_uk, w_uv, w_pool, pool_scale, g_out_attn, g_out_pool, w_o, g_norm2, w_gate, w_up, w_down,
           g_final):
    depth = w_in.shape[0]
    assert depth == 1
    l = 0
    pool_w = w_pool.shape[1] * w_pool.shape[2]
    a0 = Q_LORA_RANK
    a1 = a0 + KV_LORA_RANK
    a2 = a1 + QK_ROPE_DIM
    win = w_in[l]
    w1 = jnp.concatenate([win[:, :a1], _rope_cols(win[:, a1:a2]), win[:, a2:]], axis=1).astype(BF16)
    wq = w_uq[l].reshape(Q_LORA_RANK, N_HEADS, QK_NOPE_DIM + QK_ROPE_DIM)
    w2 = jnp.concatenate([wq[..., :QK_NOPE_DIM].reshape(Q_LORA_RANK, -1),
                          _rope_cols(wq[..., QK_NOPE_DIM:]).reshape(Q_LORA_RANK, -1)], axis=1).astype(BF16)
    eye = jnp.eye(N_HEADS, dtype=F32)
    w3 = jnp.einsum('hrd,hg->hdgr', w_uk[l], eye).reshape(N_HEADS * QK_NOPE_DIM, N_HEADS * KV_LORA_RANK)
    wuv = jnp.einsum('hrd,hg->hrgd', w_uv[l], eye).reshape(N_HEADS * KV_LORA_RANK, -1)
    row = lambda g: g.reshape(1, -1).astype(F32)
    pre_w = (row(g_norm1[l]), w1, row(g_q[l]), w2, w3.astype(BF16), row(g_kv[l]), w_pool[l].astype(BF16),
             row(pool_scale[l]), row(g_out_pool[l]))
    post_w = (wuv.astype(BF16), row(g_out_attn[l]), w_o[l].astype(BF16), row(g_norm2[l]),
              w_gate[l].astype(BF16), w_up[l].astype(BF16), w_down[l].astype(BF16), row(g_final))

    B, T, _ = x_prompt.shape
    prev_p = jnp.zeros((B, POOL_PAD, pool_w), F32)
    y_p, ckv_p, kr_p, pool_p = _layer(
        x_prompt, 0, prev_p, functools.partial(_attn_prompt_call, tq=256, tk=256), pre_w, post_w,
        tm_pre=512, tm_post=256)

    past = cache_kv_latent.shape[2]
    Ts = x_sample.shape[1]
    prev_s = jnp.pad(state_pool[l], ((0, 0), (POOL_PAD - POOL_STATE, 0), (0, 0)))
    attn_s = lambda q, k: _attn_sample_call(q, cache_kv_latent[l], cache_k_rope[l], k)
    y_s, ckv_s, kr_s, pool_s = _layer(x_sample, past, prev_s, attn_s, pre_w, post_w, tm_pre=Ts, tm_post=Ts)

    return (y_p, y_s, ckv_p[None], kr_p[None], pool_p[None], ckv_s[None], kr_s[None], pool_s[None])
```

```python
import functools
import math

import jax
import jax.numpy as jnp
from jax import lax
from jax.experimental import pallas as pl
from jax.experimental.pallas import tpu as pltpu

CHUNK = 64
N_HEADS = 8
QK_NOPE_DIM = 64
QK_ROPE_DIM = 32
Q_LORA_RANK = 256
KV_LORA_RANK = 128
ROPE_BASE = 10000.0
POOL_WINDOWS = (2, 4, 8, 16)
POOL_STATE = max(POOL_WINDOWS) - 1
POOL_PAD = POOL_STATE + 1
SM_SCALE = 1.0 / math.sqrt(QK_NOPE_DIM + QK_ROPE_DIM)
Q_SCALE = SM_SCALE * math.log2(math.e)
EPS = 1e-6
ATTN_SLAB = 64
LANES = 128
QK_PAD = 2 * LANES
ROPE_ROLL = LANES - QK_ROPE_DIM
VMEM_LIMIT = 56 * 1024 * 1024

F32 = jnp.float32
BF16 = jnp.bfloat16


def _rms(x, g):
    ms = jnp.mean(x * x, axis=-1, keepdims=True)
    return x * lax.rsqrt(ms + EPS) * g


def _rope_block(blk, cos, sin):
    return blk * cos + pltpu.roll(blk, ROPE_ROLL, 1) * sin


def _const_spec(shape):
    nd = len(shape)
    return pl.BlockSpec(shape, lambda *_: (0,) * nd, pipeline_mode=pl.Buffered(1))


def _pre_kernel(x_ref, prev_ref, inv_ref, cos_ref, sin_ref, g1_ref, w1_ref, gq_ref, w2_ref, w3_ref,
                gkv_ref, wpool_ref, pscale_ref, gpool_ref,
                q_ref, k_ref, ckv_ref, kr_ref, opool_ref, newpool_ref, ext_ref, *, tm, pool_w):
    t = pl.program_id(1)
    n_t = pl.num_programs(1)
    x = x_ref[0]
    h = _rms(x, g1_ref[...]).astype(BF16)
    z = jnp.dot(h, w1_ref[...], preferred_element_type=F32)
    a0 = Q_LORA_RANK
    a1 = a0 + KV_LORA_RANK
    a2 = a1 + LANES
    cos = cos_ref[...]
    sin = sin_ref[...]

    ckv_n = _rms(z[:, a0:a1], gkv_ref[...])
    k_rot = _rope_block(z[:, a1:a2], cos, sin)
    ckv_ref[0] = ckv_n
    kr_ref[0] = k_rot[:, :QK_ROPE_DIM]
    k_ref[0, :, 0:LANES] = ckv_n.astype(BF16)
    spare = lax.broadcasted_iota(jnp.int32, k_rot.shape, 1) >= QK_ROPE_DIM
    k_ref[0, :, LANES:QK_PAD] = jnp.where(spare, 1.0, k_rot).astype(BF16)

    cqn = _rms(z[:, :a0], gq_ref[...]).astype(BF16)
    q = jnp.dot(cqn, w2_ref[...], preferred_element_type=F32)
    n_nope = N_HEADS * QK_NOPE_DIM
    q_lat = jnp.dot(q[:, :n_nope].astype(BF16), w3_ref[...], preferred_element_type=F32)
    for hh in range(N_HEADS):
        q_ref[0, hh, :, 0:LANES] = (q_lat[:, hh * LANES:(hh + 1) * LANES] * Q_SCALE).astype(BF16)
        blk = q[:, n_nope + hh * LANES:n_nope + (hh + 1) * LANES]
        q_ref[0, hh, :, LANES:QK_PAD] = (_rope_block(blk, cos, sin) * Q_SCALE).astype(BF16)

    u = z[:, a2:]

    @pl.when(t == 0)
    def _():
        ext_ref[0:POOL_PAD, :] = prev_ref[0]

    ext_ref[POOL_PAD:POOL_PAD + tm, :] = u
    mixed = []
    for g, w in enumerate(POOL_WINDOWS):
        lo, hi = g * LANES, (g + 1) * LANES
        s = ext_ref[:, lo:hi]
        k = 1
        while k < w:
            s = s + pltpu.roll(s, k, 0)
            k *= 2
        inv_head = jnp.where(t == 0, inv_ref[:, lo:hi], 1.0 / w)
        head = s[POOL_PAD:2 * POOL_PAD] * inv_head - u[:POOL_PAD, lo:hi]
        body = s[2 * POOL_PAD:] * (1.0 / w) - u[POOL_PAD:, lo:hi]
        pooled = jnp.concatenate([head, body], axis=0).astype(BF16)
        mixed.append(jnp.dot(pooled, wpool_ref[g], preferred_element_type=F32))
    o_pool = jnp.concatenate(mixed, axis=1) * pscale_ref[...]
    opool_ref[0] = _rms(o_pool, gpool_ref[...]).astype(BF16)

    @pl.when(t == n_t - 1)
    def _():
        newpool_ref[0] = ext_ref[tm + 1:tm + POOL_PAD, :]

    ext_ref[0:POOL_PAD, :] = ext_ref[tm:tm + POOL_PAD, :]


def _pre_call(x, prev, inv_head, cos, sin, wts, *, tm):
    B, T, D = x.shape
    pool_w = prev.shape[-1]
    assert T % tm == 0 and T >= POOL_PAD and tm >= 2 * POOL_PAD and pool_w == len(POOL_WINDOWS) * LANES
    grid = (B, T // tm)
    in_specs = [
        pl.BlockSpec((1, tm, D), lambda b, t: (b, t, 0)),
        pl.BlockSpec((1, POOL_PAD, pool_w), lambda b, t: (b, 0, 0)),
        _const_spec(inv_head.shape),
        pl.BlockSpec((tm, LANES), lambda b, t: (t, 0)),
        pl.BlockSpec((tm, LANES), lambda b, t: (t, 0)),
    ] + [_const_spec(w.shape) for w in wts]
    out_shape = (
        jax.ShapeDtypeStruct((B, N_HEADS, T, QK_PAD), BF16),
        jax.ShapeDtypeStruct((B, T, QK_PAD), BF16),
        jax.ShapeDtypeStruct((B, T, KV_LORA_RANK), F32),
        jax.ShapeDtypeStruct((B, T, QK_ROPE_DIM), F32),
        jax.ShapeDtypeStruct((B, T, pool_w), BF16),
        jax.ShapeDtypeStruct((B, POOL_STATE, pool_w), F32),
    )
    out_specs = (
        pl.BlockSpec((1, N_HEADS, tm, QK_PAD), lambda b, t: (b, 0, t, 0)),
        pl.BlockSpec((1, tm, QK_PAD), lambda b, t: (b, t, 0)),
        pl.BlockSpec((1, tm, KV_LORA_RANK), lambda b, t: (b, t, 0)),
        pl.BlockSpec((1, tm, QK_ROPE_DIM), lambda b, t: (b, t, 0)),
        pl.BlockSpec((1, tm, pool_w), lambda b, t: (b, t, 0)),
        pl.BlockSpec((1, POOL_STATE, pool_w), lambda b, t: (b, 0, 0)),
    )
    return pl.pallas_call(
        functools.partial(_pre_kernel, tm=tm, pool_w=pool_w),
        grid=grid, in_specs=in_specs, out_specs=out_specs, out_shape=out_shape,
        scratch_shapes=[pltpu.VMEM((POOL_PAD + tm, pool_w), F32)],
        compiler_params=pltpu.CompilerParams(
            dimension_semantics=("arbitrary", "arbitrary"), vmem_limit_bytes=VMEM_LIMIT),
        name=f"pre_t{tm}",
    )(x, prev, inv_head, cos, sin, *wts)


def _attn_prompt_kernel(q_ref, k_ref, o_ref, s0_ref, s1_ref, mc0_ref, mc1_ref, p_ref, alpha_ref, m_ref, acc_ref,
                        *, tq, tk):
    qi = pl.program_id(1)
    rows = N_HEADS * tq
    nt = (((1,), (1,)), ((), ()))
    n_full = (qi * tq) // tk
    slabs = [slice(i * ATTN_SLAB, (i + 1) * ATTN_SLAB) for i in range(rows // ATTN_SLAB)]

    def lane_parts(s):
        return [s[:, c * LANES:(c + 1) * LANES] for c in range(tk // LANES)]

    def keys(j):
        return k_ref[0, pl.ds(pl.multiple_of(j * tk, tk), tk), :]

    def scores(j, buf, masked):
        s_ref, mc_ref = buf
        s_all = lax.dot_general(q_ref[0].reshape(rows, QK_PAD), keys(j), nt, preferred_element_type=F32)
        for i, r in enumerate(slabs):
            s = s_all[r, :]
            if masked:
                tok = qi * tq + (i * ATTN_SLAB) % tq + lax.broadcasted_iota(jnp.int32, s.shape, 0)
                key = j * tk + lax.broadcasted_iota(jnp.int32, s.shape, 1)
                s = jnp.where(key // CHUNK <= tok // CHUNK, s, -jnp.inf)
            s_ref[r, :] = s
            m_cur = jnp.max(functools.reduce(jnp.maximum, lane_parts(s)), axis=-1, keepdims=True)
            mc_ref[r, :] = jnp.broadcast_to(m_cur, (ATTN_SLAB, LANES))

    def softmax_pv(j, buf):
        s_ref, mc_ref = buf
        for r in slabs:
            m_prev = m_ref[r, :]
            m_new = jnp.maximum(m_prev, mc_ref[r, :])
            alpha_ref[r, :] = jnp.exp2(m_prev - m_new)
            m_ref[r, :] = m_new
            for c, part in enumerate(lane_parts(s_ref[r, :])):
                p_ref[r, c * LANES:(c + 1) * LANES] = jnp.exp2(part - m_new).astype(BF16)
        alpha = alpha_ref[...]
        pv = jnp.dot(p_ref[...], keys(j), preferred_element_type=F32)
        for c in range(QK_PAD // LANES):
            cols = slice(c * LANES, (c + 1) * LANES)
            acc_ref[:, cols] = alpha * acc_ref[:, cols] + pv[:, cols]

    m_ref[...] = jnp.full(m_ref.shape, -jnp.inf, F32)
    acc_ref[...] = jnp.zeros(acc_ref.shape, F32)

    buf0 = (s0_ref, mc0_ref)
    buf1 = (s1_ref, mc1_ref)

    @pl.when(n_full == 0)
    def _():
        scores(0, buf0, True)
        softmax_pv(0, buf0)

    @pl.when(n_full > 0)
    def _():
        scores(0, buf0, False)

    def pair(i, carry):
        j = 2 * i
        scores(j + 1, buf1, False)
        softmax_pv(j, buf0)
        scores(j + 2, buf0, False)
        softmax_pv(j + 1, buf1)
        return carry

    lax.fori_loop(0, jnp.maximum(n_full - 1, 0) // 2, pair, 0)

    @pl.when(n_full % 2 == 1)
    def _():
        scores(n_full, buf1, True)
        softmax_pv(n_full - 1, buf0)
        softmax_pv(n_full, buf1)

    @pl.when((n_full % 2 == 0) & (n_full > 0))
    def _():
        scores(n_full - 1, buf1, False)
        softmax_pv(n_full - 2, buf0)
        scores(n_full, buf0, True)
        softmax_pv(n_full - 1, buf1)
        softmax_pv(n_full, buf0)

    l = acc_ref[:, QK_PAD - 1:QK_PAD]
    o = (acc_ref[:, 0:KV_LORA_RANK] / l).astype(BF16)
    for hh in range(N_HEADS):
        o_ref[0, :, hh * LANES:(hh + 1) * LANES] = o[hh * tq:(hh + 1) * tq]


def _attn_prompt_call(q, k, *, tq, tk):
    B, H, T, _ = q.shape
    assert T % tq == 0 and T % tk == 0 and tq % CHUNK == 0 and tq % ATTN_SLAB == 0 and tk % LANES == 0
    rows = H * tq
    return pl.pallas_call(
        functools.partial(_attn_prompt_kernel, tq=tq, tk=tk),
        grid=(B, T // tq),
        in_specs=[
            pl.BlockSpec((1, H, tq, QK_PAD), lambda b, i: (b, 0, i, 0)),
            pl.BlockSpec((1, T, QK_PAD), lambda b, i: (b, 0, 0)),
        ],
        out_specs=pl.BlockSpec((1, tq, H * KV_LORA_RANK), lambda b, i: (b, i, 0)),
        out_shape=jax.ShapeDtypeStruct((B, T, H * KV_LORA_RANK), BF16),
        scratch_shapes=[pltpu.VMEM((rows, tk), F32), pltpu.VMEM((rows, tk), F32),
                        pltpu.VMEM((rows, LANES), F32), pltpu.VMEM((rows, LANES), F32),
                        pltpu.VMEM((rows, tk), BF16),
                        pltpu.VMEM((rows, LANES), F32),
                        pltpu.VMEM((rows, LANES), F32),
                        pltpu.VMEM((rows, QK_PAD), F32)],
        compiler_params=pltpu.CompilerParams(
            dimension_semantics=("arbitrary", "arbitrary"), vmem_limit_bytes=VMEM_LIMIT),
        name="attn_prompt",
    )(q, k)


def _attn_sample_kernel(q_ref, ckv_ref, ckr_ref, knew_ref, o_ref, *, tq, past):
    rows = N_HEADS * tq
    q = q_ref[0].reshape(rows, QK_PAD)
    q_lat = q[:, :KV_LORA_RANK]
    q_rope = q[:, KV_LORA_RANK:KV_LORA_RANK + QK_ROPE_DIM]
    nt = (((1,), (1,)), ((), ()))
    v_past = ckv_ref[0].astype(BF16)
    s_past = (lax.dot_general(q_lat, v_past, nt, preferred_element_type=F32)
              + lax.dot_general(q_rope, ckr_ref[0].astype(BF16), nt, preferred_element_type=F32))
    k_new = knew_ref[0]
    s_new = lax.dot_general(q, k_new, nt, preferred_element_type=F32)

    def masked(s, k_start):
        q_pos = past + lax.broadcasted_iota(jnp.int32, s.shape, 0) % tq
        k_pos = k_start + lax.broadcasted_iota(jnp.int32, s.shape, 1)
        return jnp.where(k_pos // CHUNK <= q_pos // CHUNK, s, -jnp.inf)

    s_past = masked(s_past, 0)
    s_new = masked(s_new, past)
    m = jnp.maximum(jnp.max(s_past, axis=-1, keepdims=True), jnp.max(s_new, axis=-1, keepdims=True))
    p_past = jnp.exp2(s_past - m)
    p_new = jnp.exp2(s_new - m)
    l = jnp.sum(p_past, axis=-1, keepdims=True) + jnp.sum(p_new, axis=-1, keepdims=True)
    acc = (jnp.dot(p_past.astype(BF16), v_past, preferred_element_type=F32)
           + jnp.dot(p_new.astype(BF16), k_new[:, :KV_LORA_RANK], preferred_element_type=F32))
    o = (acc / l).astype(BF16)
    for hh in range(N_HEADS):
        o_ref[0, :, hh * LANES:(hh + 1) * LANES] = o[hh * tq:(hh + 1) * tq]


def _attn_sample_call(q, cache_kv, cache_kr, k_new):
    B, H, tq, _ = q.shape
    past = cache_kv.shape[1]
    return pl.pallas_call(
        functools.partial(_attn_sample_kernel, tq=tq, past=past),
        grid=(B,),
        in_specs=[
            pl.BlockSpec((1, H, tq, QK_PAD), lambda b: (b, 0, 0, 0)),
            pl.BlockSpec((1, past, KV_LORA_RANK), lambda b: (b, 0, 0)),
            pl.BlockSpec((1, past, QK_ROPE_DIM), lambda b: (b, 0, 0)),
            pl.BlockSpec((1, tq, QK_PAD), lambda b: (b, 0, 0)),
        ],
        out_specs=pl.BlockSpec((1, tq, H * KV_LORA_RANK), lambda b: (b, 0, 0)),
        out_shape=jax.ShapeDtypeStruct((B, tq, H * KV_LORA_RANK), BF16),
        compiler_params=pltpu.CompilerParams(
            dimension_semantics=("arbitrary",), vmem_limit_bytes=VMEM_LIMIT),
        name="attn_sample",
    )(q, cache_kv, cache_kr, k_new)


def _post_kernel(x_ref, olat_ref, opool_ref, wuv_ref, gattn_ref, wo_ref, g2_ref, wg_ref, wu_ref, wd_ref,
                 gf_ref, y_ref):
    o_attn = jnp.dot(olat_ref[0], wuv_ref[...], preferred_element_type=F32)
    a_n = _rms(o_attn, gattn_ref[...]).astype(BF16)
    merged = jnp.concatenate([a_n, opool_ref[0]], axis=1)
    x1 = x_ref[0] + jnp.dot(merged, wo_ref[...], preferred_element_type=F32)
    h2 = _rms(x1, g2_ref[...]).astype(BF16)
    gate = jnp.dot(h2, wg_ref[...], preferred_element_type=F32)
    up = jnp.dot(h2, wu_ref[...], preferred_element_type=F32)
    act = (gate * jax.nn.sigmoid(gate) * up).astype(BF16)
    x2 = x1 + jnp.dot(act, wd_ref[...], preferred_element_type=F32)
    y_ref[0] = _rms(x2, gf_ref[...])


def _post_call(x, o_lat, o_pool, wts, *, tm):
    B, T, D = x.shape
    assert T % tm == 0
    in_specs = [
        pl.BlockSpec((1, tm, D), lambda b, t: (b, t, 0)),
        pl.BlockSpec((1, tm, o_lat.shape[-1]), lambda b, t: (b, t, 0)),
        pl.BlockSpec((1, tm, o_pool.shape[-1]), lambda b, t: (b, t, 0)),
    ] + [_const_spec(w.shape) for w in wts]
    return pl.pallas_call(
        _post_kernel,
        grid=(B, T // tm),
        in_specs=in_specs,
        out_specs=pl.BlockSpec((1, tm, D), lambda b, t: (b, t, 0)),
        out_shape=jax.ShapeDtypeStruct((B, T, D), F32),
        compiler_params=pltpu.CompilerParams(
            dimension_semantics=("arbitrary", "arbitrary"), vmem_limit_bytes=VMEM_LIMIT),
        name=f"post_t{tm}",
    )(x, o_lat, o_pool, *wts)


def _swap_halves(w):
    half = w.shape[-1] // 2
    return jnp.concatenate([-w[..., half:], w[..., :half]], axis=-1)


def _rope_cols(w):
    pad = jnp.zeros(w.shape[:-1] + (LANES - 2 * QK_ROPE_DIM,), w.dtype)
    return jnp.concatenate([w, _swap_halves(w), pad], axis=-1)


def _rope_tables(pos):
    d = QK_ROPE_DIM
    freqs = jnp.power(ROPE_BASE, -jnp.arange(0, d, 2, dtype=F32) / d)
    ang = pos.astype(F32)[:, None] * freqs[None, :]
    pad = jnp.zeros((pos.shape[0], LANES - d), F32)
    cos = jnp.concatenate([jnp.cos(ang), jnp.cos(ang), pad], axis=-1)
    sin = jnp.concatenate([jnp.sin(ang), jnp.sin(ang), pad], axis=-1)
    return cos, sin


def _pool_inv_head(pos0, pool_w):
    pos = pos0 + jnp.arange(POOL_PAD)
    win = jnp.repeat(jnp.asarray(POOL_WINDOWS, jnp.int32), pool_w // len(POOL_WINDOWS))
    return 1.0 / jnp.minimum(pos[:, None] + 1, win[None, :]).astype(F32)


def _layer(x, pos0, prev, attn_fn, pre_w, post_w, *, tm_pre, tm_post):
    T = x.shape[1]
    cos, sin = _rope_tables(pos0 + jnp.arange(T))
    inv_head = _pool_inv_head(pos0, prev.shape[-1])
    q, k, ckv, kr, o_pool, new_pool = _pre_call(x, prev, inv_head, cos, sin, pre_w, tm=tm_pre)
    o_lat = attn_fn(q, k)
    y = _post_call(x, o_lat, o_pool, post_w, tm=tm_post)
    return y, ckv, kr, new_pool


def kernel(x_prompt, x_sample, cache_kv_latent, cache_k_rope, state_pool, g_norm1, w_in, g_q, w_uq, g_kv,
           w_uk, w_uv, w_pool, pool_scale, g_out_attn, g_out_pool, w_o, g_norm2, w_gate, w_up, w_down,
           g_final):
    depth = w_in.shape[0]
    assert depth == 1
    l = 0
    pool_w = w_pool.shape[1] * w_pool.shape[2]
    a0 = Q_LORA_RANK
    a1 = a0 + KV_LORA_RANK
    a2 = a1 + QK_ROPE_DIM
    win = w_in[l]
    w1 = jnp.concatenate([win[:, :a1], _rope_cols(win[:, a1:a2]), win[:, a2:]], axis=1).astype(BF16)
    wq = w_uq[l].reshape(Q_LORA_RANK, N_HEADS, QK_NOPE_DIM + QK_ROPE_DIM)
    w2 = jnp.concatenate([wq[..., :QK_NOPE_DIM].reshape(Q_LORA_RANK, -1),
                          _rope_cols(wq[..., QK_NOPE_DIM:]).reshape(Q_LORA_RANK, -1)], axis=1).astype(BF16)
    eye = jnp.eye(N_HEADS, dtype=F32)
    w3 = jnp.einsum('hrd,hg->hdgr', w_uk[l], eye).reshape(N_HEADS * QK_NOPE_DIM, N_HEADS * KV_LORA_RANK)
    wuv = jnp.einsum('hrd,hg->hrgd', w_uv[l], eye).reshape(N_HEADS * KV_LORA_RANK, -1)
    row = lambda g: g.reshape(1, -1).astype(F32)
    pre_w = (row(g_norm1[l]), w1, row(g_q[l]), w2, w3.astype(BF16), row(g_kv[l]), w_pool[l].astype(BF16),
             row(pool_scale[l]), row(g_out_pool[l]))
    post_w = (wuv.astype(BF16), row(g_out_attn[l]), w_o[l].astype(BF16), row(g_norm2[l]),
              w_gate[l].astype(BF16), w_up[l].astype(BF16), w_down[l].astype(BF16), row(g_final))

    B, T, _ = x_prompt.shape
    prev_p = jnp.zeros((B, POOL_PAD, pool_w), F32)
    y_p, ckv_p, kr_p, pool_p = _layer(
        x_prompt, 0, prev_p, functools.partial(_attn_prompt_call, tq=256, tk=512), pre_w, post_w,
        tm_pre=512, tm_post=256)

    past = cache_kv_latent.shape[2]
    Ts = x_sample.shape[1]
    prev_s = jnp.pad(state_pool[l], ((0, 0), (POOL_PAD - POOL_STATE, 0), (0, 0)))
    attn_s = lambda q, k: _attn_sample_call(q, cache_kv_latent[l], cache_k_rope[l], k)
    y_s, ckv_s, kr_s, pool_s = _layer(x_sample, past, prev_s, attn_s, pre_w, post_w, tm_pre=Ts, tm_post=Ts)

    return (y_p, y_s, ckv_p[None], kr_p[None], pool_p[None], ckv_s[None], kr_s[None], pool_s[None])
```

```python
import functools
import math

import jax
import jax.numpy as jnp
from jax import lax
from jax.experimental import pallas as pl
from jax.experimental.pallas import tpu as pltpu

CHUNK = 64
N_HEADS = 8
QK_NOPE_DIM = 64
QK_ROPE_DIM = 32
Q_LORA_RANK = 256
KV_LORA_RANK = 128
ROPE_BASE = 10000.0
POOL_WINDOWS = (2, 4, 8, 16)
POOL_STATE = max(POOL_WINDOWS) - 1
POOL_PAD = POOL_STATE + 1
SM_SCALE = 1.0 / math.sqrt(QK_NOPE_DIM + QK_ROPE_DIM)
Q_SCALE = SM_SCALE * math.log2(math.e)
EPS = 1e-6
ATTN_SLAB = 64
LANES = 128
QK_PAD = 2 * LANES
ROPE_ROLL = LANES - QK_ROPE_DIM
VMEM_LIMIT = 56 * 1024 * 1024

F32 = jnp.float32
BF16 = jnp.bfloat16


def _rms(x, g):
    ms = jnp.mean(x * x, axis=-1, keepdims=True)
    return x * lax.rsqrt(ms + EPS) * g


def _rope_block(blk, cos, sin):
    return blk * cos + pltpu.roll(blk, ROPE_ROLL, 1) * sin


def _const_spec(shape):
    nd = len(shape)
    return pl.BlockSpec(shape, lambda *_: (0,) * nd, pipeline_mode=pl.Buffered(1))


def _pre_kernel(x_ref, prev_ref, inv_ref, cos_ref, sin_ref, g1_ref, w1_ref, gq_ref, w2_ref, w3_ref,
                gkv_ref, wpool_ref, pscale_ref, gpool_ref,
                q_ref, k_ref, ckv_ref, kr_ref, opool_ref, newpool_ref, ext_ref, *, tm, pool_w):
    t = pl.program_id(1)
    n_t = pl.num_programs(1)
    x = x_ref[0]
    h = _rms(x, g1_ref[...]).astype(BF16)
    z = jnp.dot(h, w1_ref[...], preferred_element_type=F32)
    a0 = Q_LORA_RANK
    a1 = a0 + KV_LORA_RANK
    a2 = a1 + LANES
    cos = cos_ref[...]
    sin = sin_ref[...]

    ckv_n = _rms(z[:, a0:a1], gkv_ref[...])
    k_rot = _rope_block(z[:, a1:a2], cos, sin)
    ckv_ref[0] = ckv_n
    kr_ref[0] = k_rot[:, :QK_ROPE_DIM]
    k_ref[0, :, 0:LANES] = ckv_n.astype(BF16)
    spare = lax.broadcasted_iota(jnp.int32, k_rot.shape, 1) >= QK_ROPE_DIM
    k_ref[0, :, LANES:QK_PAD] = jnp.where(spare, 1.0, k_rot).astype(BF16)

    cqn = _rms(z[:, :a0], gq_ref[...]).astype(BF16)
    q = jnp.dot(cqn, w2_ref[...], preferred_element_type=F32)
    n_nope = N_HEADS * QK_NOPE_DIM
    q_lat = jnp.dot(q[:, :n_nope].astype(BF16), w3_ref[...], preferred_element_type=F32)
    for hh in range(N_HEADS):
        q_ref[0, hh, :, 0:LANES] = (q_lat[:, hh * LANES:(hh + 1) * LANES] * Q_SCALE).astype(BF16)
        blk = q[:, n_nope + hh * LANES:n_nope + (hh + 1) * LANES]
        q_ref[0, hh, :, LANES:QK_PAD] = (_rope_block(blk, cos, sin) * Q_SCALE).astype(BF16)

    u = z[:, a2:]

    @pl.when(t == 0)
    def _():
        ext_ref[0:POOL_PAD, :] = prev_ref[0]

    ext_ref[POOL_PAD:POOL_PAD + tm, :] = u
    mixed = []
    for g, w in enumerate(POOL_WINDOWS):
        lo, hi = g * LANES, (g + 1) * LANES
        s = ext_ref[:, lo:hi]
        k = 1
        while k < w:
            s = s + pltpu.roll(s, k, 0)
            k *= 2
        inv_head = jnp.where(t == 0, inv_ref[:, lo:hi], 1.0 / w)
        head = s[POOL_PAD:2 * POOL_PAD] * inv_head - u[:POOL_PAD, lo:hi]
        body = s[2 * POOL_PAD:] * (1.0 / w) - u[POOL_PAD:, lo:hi]
        pooled = jnp.concatenate([head, body], axis=0).astype(BF16)
        mixed.append(jnp.dot(pooled, wpool_ref[g], preferred_element_type=F32))
    o_pool = jnp.concatenate(mixed, axis=1) * pscale_ref[...]
    opool_ref[0] = _rms(o_pool, gpool_ref[...]).astype(BF16)

    @pl.when(t == n_t - 1)
    def _():
        newpool_ref[0] = ext_ref[tm + 1:tm + POOL_PAD, :]

    ext_ref[0:POOL_PAD, :] = ext_ref[tm:tm + POOL_PAD, :]


def _pre_call(x, prev, inv_head, cos, sin, wts, *, tm):
    B, T, D = x.shape
    pool_w = prev.shape[-1]
    assert T % tm == 0 and T >= POOL_PAD and tm >= 2 * POOL_PAD and pool_w == len(POOL_WINDOWS) * LANES
    grid = (B, T // tm)
    in_specs = [
        pl.BlockSpec((1, tm, D), lambda b, t: (b, t, 0)),
        pl.BlockSpec((1, POOL_PAD, pool_w), lambda b, t: (b, 0, 0)),
        _const_spec(inv_head.shape),
        pl.BlockSpec((tm, LANES), lambda b, t: (t, 0)),
        pl.BlockSpec((tm, LANES), lambda b, t: (t, 0)),
    ] + [_const_spec(w.shape) for w in wts]
    out_shape = (
        jax.ShapeDtypeStruct((B, N_HEADS, T, QK_PAD), BF16),
        jax.ShapeDtypeStruct((B, T, QK_PAD), BF16),
        jax.ShapeDtypeStruct((B, T, KV_LORA_RANK), F32),
        jax.ShapeDtypeStruct((B, T, QK_ROPE_DIM), F32),
        jax.ShapeDtypeStruct((B, T, pool_w), BF16),
        jax.ShapeDtypeStruct((B, POOL_STATE, pool_w), F32),
    )
    out_specs = (
        pl.BlockSpec((1, N_HEADS, tm, QK_PAD), lambda b, t: (b, 0, t, 0)),
        pl.BlockSpec((1, tm, QK_PAD), lambda b, t: (b, t, 0)),
        pl.BlockSpec((1, tm, KV_LORA_RANK), lambda b, t: (b, t, 0)),
        pl.BlockSpec((1, tm, QK_ROPE_DIM), lambda b, t: (b, t, 0)),
        pl.BlockSpec((1, tm, pool_w), lambda b, t: (b, t, 0)),
        pl.BlockSpec((1, POOL_STATE, pool_w), lambda b, t: (b, 0, 0)),
    )
    return pl.pallas_call(
        functools.partial(_pre_kernel, tm=tm, pool_w=pool_w),
        grid=grid, in_specs=in_specs, out_specs=out_specs, out_shape=out_shape,
        scratch_shapes=[pltpu.VMEM((POOL_PAD + tm, pool_w), F32)],
        compiler_params=pltpu.CompilerParams(
            dimension_semantics=("arbitrary", "arbitrary"), vmem_limit_bytes=VMEM_LIMIT),
        name=f"pre_t{tm}",
    )(x, prev, inv_head, cos, sin, *wts)


def _attn_prompt_kernel(q_ref, k_ref, o_ref, s0_ref, s1_ref, mc0_ref, mc1_ref, p_ref, alpha_ref, m_ref, acc_ref,
                        *, tq, tk):
    qi = pl.program_id(1)
    rows = N_HEADS * tq
    nt = (((1,), (1,)), ((), ()))
    n_full = (qi * tq) // tk
    slabs = [slice(i * ATTN_SLAB, (i + 1) * ATTN_SLAB) for i in range(rows // ATTN_SLAB)]

    def lane_parts(s):
        return [s[:, c * LANES:(c + 1) * LANES] for c in range(tk // LANES)]

    def keys(j):
        return k_ref[0, pl.ds(pl.multiple_of(j * tk, tk), tk), :]

    def scores(j, buf, masked):
        s_ref, mc_ref = buf
        s_all = lax.dot_general(q_ref[0].reshape(rows, QK_PAD), keys(j), nt, preferred_element_type=F32)
        for i, r in enumerate(slabs):
            s = s_all[r, :]
            if masked:
                tok = qi * tq + (i * ATTN_SLAB) % tq + lax.broadcasted_iota(jnp.int32, s.shape, 0)
                key = j * tk + lax.broadcasted_iota(jnp.int32, s.shape, 1)
                s = jnp.where(key // CHUNK <= tok // CHUNK, s, -jnp.inf)
            s_ref[r, :] = s
            m_cur = jnp.max(functools.reduce(jnp.maximum, lane_parts(s)), axis=-1, keepdims=True)
            mc_ref[r, :] = jnp.broadcast_to(m_cur, (ATTN_SLAB, LANES))

    def softmax_pv(j, buf):
        s_ref, mc_ref = buf
        for r in slabs:
            m_prev = m_ref[r, :]
            m_new = jnp.maximum(m_prev, mc_ref[r, :])
            alpha_ref[r, :] = jnp.exp2(m_prev - m_new)
            m_ref[r, :] = m_new
            for c, part in enumerate(lane_parts(s_ref[r, :])):
                p_ref[r, c * LANES:(c + 1) * LANES] = jnp.exp2(part - m_new).astype(BF16)
        alpha = alpha_ref[...]
        pv = jnp.dot(p_ref[...], keys(j), preferred_element_type=F32)
        for c in range(QK_PAD // LANES):
            cols = slice(c * LANES, (c + 1) * LANES)
            acc_ref[:, cols] = alpha * acc_ref[:, cols] + pv[:, cols]

    m_ref[...] = jnp.full(m_ref.shape, -jnp.inf, F32)
    acc_ref[...] = jnp.zeros(acc_ref.shape, F32)

    bufs = ((s0_ref, mc0_ref), (s1_ref, mc1_ref))

    @pl.when(n_full == 0)
    def _():
        scores(0, bufs[0], True)
        softmax_pv(0, bufs[0])

    @pl.when(n_full > 0)
    def _():
        scores(0, bufs[0], False)

    def step(j, carry):
        for par in range(2):
            @pl.when(j % 2 == par)
            def _():
                scores(j + 1, bufs[1 - par], False)
                softmax_pv(j, bufs[par])
        return carry

    lax.fori_loop(0, n_full - 1, step, 0)

    for par in range(2):
        @pl.when((n_full > 0) & (n_full % 2 == par))
        def _():
            scores(n_full, bufs[par], True)
            softmax_pv(n_full - 1, bufs[1 - par])

        @pl.when((n_full > 0) & (n_full % 2 == par))
        def _():
            softmax_pv(n_full, bufs[par])

    l = acc_ref[:, QK_PAD - 1:QK_PAD]
    o = (acc_ref[:, 0:KV_LORA_RANK] / l).astype(BF16)
    for hh in range(N_HEADS):
        o_ref[0, :, hh * LANES:(hh + 1) * LANES] = o[hh * tq:(hh + 1) * tq]


def _attn_prompt_call(q, k, *, tq, tk):
    B, H, T, _ = q.shape
    assert T % tq == 0 and T % tk == 0 and tq % CHUNK == 0 and tq % ATTN_SLAB == 0 and tk % LANES == 0
    rows = H * tq
    return pl.pallas_call(
        functools.partial(_attn_prompt_kernel, tq=tq, tk=tk),
        grid=(B, T // tq),
        in_specs=[
            pl.BlockSpec((1, H, tq, QK_PAD), lambda b, i: (b, 0, i, 0)),
            pl.BlockSpec((1, T, QK_PAD), lambda b, i: (b, 0, 0)),
        ],
        out_specs=pl.BlockSpec((1, tq, H * KV_LORA_RANK), lambda b, i: (b, i, 0)),
        out_shape=jax.ShapeDtypeStruct((B, T, H * KV_LORA_RANK), BF16),
        scratch_shapes=[pltpu.VMEM((rows, tk), F32), pltpu.VMEM((rows, tk), F32),
                        pltpu.VMEM((rows, LANES), F32), pltpu.VMEM((rows, LANES), F32),
                        pltpu.VMEM((rows, tk), BF16),
                        pltpu.VMEM((rows, LANES), F32),
                        pltpu.VMEM((rows, LANES), F32),
                        pltpu.VMEM((rows, QK_PAD), F32)],
        compiler_params=pltpu.CompilerParams(
            dimension_semantics=("arbitrary", "arbitrary"), vmem_limit_bytes=VMEM_LIMIT),
        name="attn_prompt",
    )(q, k)


def _attn_sample_kernel(q_ref, ckv_ref, ckr_ref, knew_ref, o_ref, *, tq, past):
    rows = N_HEADS * tq
    q = q_ref[0].reshape(rows, QK_PAD)
    q_lat = q[:, :KV_LORA_RANK]
    q_rope = q[:, KV_LORA_RANK:KV_LORA_RANK + QK_ROPE_DIM]
    nt = (((1,), (1,)), ((), ()))
    v_past = ckv_ref[0].astype(BF16)
    s_past = (lax.dot_general(q_lat, v_past, nt, preferred_element_type=F32)
              + lax.dot_general(q_rope, ckr_ref[0].astype(BF16), nt, preferred_element_type=F32))
    k_new = knew_ref[0]
    s_new = lax.dot_general(q, k_new, nt, preferred_element_type=F32)

    def masked(s, k_start):
        q_pos = past + lax.broadcasted_iota(jnp.int32, s.shape, 0) % tq
        k_pos = k_start + lax.broadcasted_iota(jnp.int32, s.shape, 1)
        return jnp.where(k_pos // CHUNK <= q_pos // CHUNK, s, -jnp.inf)

    s_past = masked(s_past, 0)
    s_new = masked(s_new, past)
    m = jnp.maximum(jnp.max(s_past, axis=-1, keepdims=True), jnp.max(s_new, axis=-1, keepdims=True))
    p_past = jnp.exp2(s_past - m)
    p_new = jnp.exp2(s_new - m)
    l = jnp.sum(p_past, axis=-1, keepdims=True) + jnp.sum(p_new, axis=-1, keepdims=True)
    acc = (jnp.dot(p_past.astype(BF16), v_past, preferred_element_type=F32)
           + jnp.dot(p_new.astype(BF16), k_new[:, :KV_LORA_RANK], preferred_element_type=F32))
    o = (acc / l).astype(BF16)
    for hh in range(N_HEADS):
        o_ref[0, :, hh * LANES:(hh + 1) * LANES] = o[hh * tq:(hh + 1) * tq]


def _attn_sample_call(q, cache_kv, cache_kr, k_new):
    B, H, tq, _ = q.shape
    past = cache_kv.shape[1]
    return pl.pallas_call(
        functools.partial(_attn_sample_kernel, tq=tq, past=past),
        grid=(B,),
        in_specs=[
            pl.BlockSpec((1, H, tq, QK_PAD), lambda b: (b, 0, 0, 0)),
            pl.BlockSpec((1, past, KV_LORA_RANK), lambda b: (b, 0, 0)),
            pl.BlockSpec((1, past, QK_ROPE_DIM), lambda b: (b, 0, 0)),
            pl.BlockSpec((1, tq, QK_PAD), lambda b: (b, 0, 0)),
        ],
        out_specs=pl.BlockSpec((1, tq, H * KV_LORA_RANK), lambda b: (b, 0, 0)),
        out_shape=jax.ShapeDtypeStruct((B, tq, H * KV_LORA_RANK), BF16),
        compiler_params=pltpu.CompilerParams(
            dimension_semantics=("arbitrary",), vmem_limit_bytes=VMEM_LIMIT),
        name="attn_sample",
    )(q, cache_kv, cache_kr, k_new)


def _post_kernel(x_ref, olat_ref, opool_ref, wuv_ref, gattn_ref, wo_ref, g2_ref, wg_ref, wu_ref, wd_ref,
                 gf_ref, y_ref):
    o_attn = jnp.dot(olat_ref[0], wuv_ref[...], preferred_element_type=F32)
    a_n = _rms(o_attn, gattn_ref[...]).astype(BF16)
    merged = jnp.concatenate([a_n, opool_ref[0]], axis=1)
    x1 = x_ref[0] + jnp.dot(merged, wo_ref[...], preferred_element_type=F32)
    h2 = _rms(x1, g2_ref[...]).astype(BF16)
    gate = jnp.dot(h2, wg_ref[...], preferred_element_type=F32)
    up = jnp.dot(h2, wu_ref[...], preferred_element_type=F32)
    act = (gate * jax.nn.sigmoid(gate) * up).astype(BF16)
    x2 = x1 + jnp.dot(act, wd_ref[...], preferred_element_type=F32)
    y_ref[0] = _rms(x2, gf_ref[...])


def _post_call(x, o_lat, o_pool, wts, *, tm):
    B, T, D = x.shape
    assert T % tm == 0
    in_specs = [
        pl.BlockSpec((1, tm, D), lambda b, t: (b, t, 0)),
        pl.BlockSpec((1, tm, o_lat.shape[-1]), lambda b, t: (b, t, 0)),
        pl.BlockSpec((1, tm, o_pool.shape[-1]), lambda b, t: (b, t, 0)),
    ] + [_const_spec(w.shape) for w in wts]
    return pl.pallas_call(
        _post_kernel,
        grid=(B, T // tm),
        in_specs=in_specs,
        out_specs=pl.BlockSpec((1, tm, D), lambda b, t: (b, t, 0)),
        out_shape=jax.ShapeDtypeStruct((B, T, D), F32),
        compiler_params=pltpu.CompilerParams(
            dimension_semantics=("arbitrary", "arbitrary"), vmem_limit_bytes=VMEM_LIMIT),
        name=f"post_t{tm}",
    )(x, o_lat, o_pool, *wts)


def _swap_halves(w):
    half = w.shape[-1] // 2
    return jnp.concatenate([-w[..., half:], w[..., :half]], axis=-1)


def _rope_cols(w):
    pad = jnp.zeros(w.shape[:-1] + (LANES - 2 * QK_ROPE_DIM,), w.dtype)
    return jnp.concatenate([w, _swap_halves(w), pad], axis=-1)


def _rope_tables(pos):
    d = QK_ROPE_DIM
    freqs = jnp.power(ROPE_BASE, -jnp.arange(0, d, 2, dtype=F32) / d)
    ang = pos.astype(F32)[:, None] * freqs[None, :]
    pad = jnp.zeros((pos.shape[0], LANES - d), F32)
    cos = jnp.concatenate([jnp.cos(ang), jnp.cos(ang), pad], axis=-1)
    sin = jnp.concatenate([jnp.sin(ang), jnp.sin(ang), pad], axis=-1)
    return cos, sin


def _pool_inv_head(pos0, pool_w):
    pos = pos0 + jnp.arange(POOL_PAD)
    win = jnp.repeat(jnp.asarray(POOL_WINDOWS, jnp.int32), pool_w // len(POOL_WINDOWS))
    return 1.0 / jnp.minimum(pos[:, None] + 1, win[None, :]).astype(F32)


def _layer(x, pos0, prev, attn_fn, pre_w, post_w, *, tm_pre, tm_post):
    T = x.shape[1]
    cos, sin = _rope_tables(pos0 + jnp.arange(T))
    inv_head = _pool_inv_head(pos0, prev.shape[-1])
    q, k, ckv, kr, o_pool, new_pool = _pre_call(x, prev, inv_head, cos, sin, pre_w, tm=tm_pre)
    o_lat = attn_fn(q, k)
    y = _post_call(x, o_lat, o_pool, post_w, tm=tm_post)
    return y, ckv, kr, new_pool


def kernel(x_prompt, x_sample, cache_kv_latent, cache_k_rope, state_pool, g_norm1, w_in, g_q, w_uq, g_kv,
           w_uk, w_uv, w_pool, pool_scale, g_out_attn, g_out_pool, w_o, g_norm2, w_gate, w_up, w_down,
           g_final):
    depth = w_in.shape[0]
    assert depth == 1
    l = 0
    pool_w = w_pool.shape[1] * w_pool.shape[2]
    a0 = Q_LORA_RANK
    a1 = a0 + KV_LORA_RANK
    a2 = a1 + QK_ROPE_DIM
    win = w_in[l]
    w1 = jnp.concatenate([win[:, :a1], _rope_cols(win[:, a1:a2]), win[:, a2:]], axis=1).astype(BF16)
    wq = w_uq[l].reshape(Q_LORA_RANK, N_HEADS, QK_NOPE_DIM + QK_ROPE_DIM)
    w2 = jnp.concatenate([wq[..., :QK_NOPE_DIM].reshape(Q_LORA_RANK, -1),
                          _rope_cols(wq[..., QK_NOPE_DIM:]).reshape(Q_LORA_RANK, -1)], axis=1).astype(BF16)
    eye = jnp.eye(N_HEADS, dtype=F32)
    w3 = jnp.einsum('hrd,hg->hdgr', w_uk[l], eye).reshape(N_HEADS * QK_NOPE_DIM, N_HEADS * KV_LORA_RANK)
    wuv = jnp.einsum('hrd,hg->hrgd', w_uv[l], eye).reshape(N_HEADS * KV_LORA_RANK, -1)
    row = lambda g: g.reshape(1, -1).astype(F32)
    pre_w = (row(g_norm1[l]), w1, row(g_q[l]), w2, w3.astype(BF16), row(g_kv[l]), w_pool[l].astype(BF16),
             row(pool_scale[l]), row(g_out_pool[l]))
    post_w = (wuv.astype(BF16), row(g_out_attn[l]), w_o[l].astype(BF16), row(g_norm2[l]),
              w_gate[l].astype(BF16), w_up[l].astype(BF16), w_down[l].astype(BF16), row(g_final))

    B, T, _ = x_prompt.shape
    prev_p = jnp.zeros((B, POOL_PAD, pool_w), F32)
    y_p, ckv_p, kr_p, pool_p = _layer(
        x_prompt, 0, prev_p, functools.partial(_attn_prompt_call, tq=256, tk=512), pre_w, post_w,
        tm_pre=512, tm_post=256)

    past = cache_kv_latent.shape[2]
    Ts = x_sample.shape[1]
    prev_s = jnp.pad(state_pool[l], ((0, 0), (POOL_PAD - POOL_STATE, 0), (0, 0)))
    attn_s = lambda q, k: _attn_sample_call(q, cache_kv_latent[l], cache_k_rope[l], k)
    y_s, ckv_s, kr_s, pool_s = _layer(x_sample, past, prev_s, attn_s, pre_w, post_w, tm_pre=Ts, tm_post=Ts)

    return (y_p, y_s, ckv_p[None], kr_p[None], pool_p[None], ckv_s[None], kr_s[None], pool_s[None])
```

```python
import functools
import math

import jax
import jax.numpy as jnp
from jax import lax
from jax.experimental import pallas as pl
from jax.experimental.pallas import tpu as pltpu

CHUNK = 64
N_HEADS = 8
QK_NOPE_DIM = 64
QK_ROPE_DIM = 32
Q_LORA_RANK = 256
KV_LORA_RANK = 128
ROPE_BASE = 10000.0
POOL_WINDOWS = (2, 4, 8, 16)
POOL_STATE = max(POOL_WINDOWS) - 1
POOL_PAD = POOL_STATE + 1
SM_SCALE = 1.0 / math.sqrt(QK_NOPE_DIM + QK_ROPE_DIM)
Q_SCALE = SM_SCALE * math.log2(math.e)
EPS = 1e-6
ATTN_SLAB = 64
LANES = 128
QK_PAD = 2 * LANES
ROPE_ROLL = LANES - QK_ROPE_DIM
VMEM_LIMIT = 56 * 1024 * 1024

F32 = jnp.float32
BF16 = jnp.bfloat16


def _rms(x, g):
    ms = jnp.mean(x * x, axis=-1, keepdims=True)
    return x * lax.rsqrt(ms + EPS) * g


def _rope_block(blk, cos, sin):
    return blk * cos + pltpu.roll(blk, ROPE_ROLL, 1) * sin


def _const_spec(shape):
    nd = len(shape)
    return pl.BlockSpec(shape, lambda *_: (0,) * nd, pipeline_mode=pl.Buffered(1))


def _pre_kernel(x_ref, prev_ref, inv_ref, cos_ref, sin_ref, g1_ref, w1_ref, gq_ref, w2_ref, w3_ref,
                gkv_ref, wpool_ref, pscale_ref, gpool_ref,
                q_ref, k_ref, ckv_ref, kr_ref, opool_ref, newpool_ref, ext_ref, *, tm, pool_w):
    t = pl.program_id(1)
    n_t = pl.num_programs(1)
    x = x_ref[0]
    h = _rms(x, g1_ref[...]).astype(BF16)
    z = jnp.dot(h, w1_ref[...], preferred_element_type=F32)
    a0 = Q_LORA_RANK
    a1 = a0 + KV_LORA_RANK
    a2 = a1 + LANES
    cos = cos_ref[...]
    sin = sin_ref[...]

    ckv_n = _rms(z[:, a0:a1], gkv_ref[...])
    k_rot = _rope_block(z[:, a1:a2], cos, sin)
    ckv_ref[0] = ckv_n
    kr_ref[0] = k_rot[:, :QK_ROPE_DIM]
    k_ref[0, :, 0:LANES] = ckv_n.astype(BF16)
    spare = lax.broadcasted_iota(jnp.int32, k_rot.shape, 1) >= QK_ROPE_DIM
    k_ref[0, :, LANES:QK_PAD] = jnp.where(spare, 1.0, k_rot).astype(BF16)

    cqn = _rms(z[:, :a0], gq_ref[...]).astype(BF16)
    q = jnp.dot(cqn, w2_ref[...], preferred_element_type=F32)
    n_nope = N_HEADS * QK_NOPE_DIM
    q_lat = jnp.dot(q[:, :n_nope].astype(BF16), w3_ref[...], preferred_element_type=F32)
    for hh in range(N_HEADS):
        q_ref[0, hh, :, 0:LANES] = (q_lat[:, hh * LANES:(hh + 1) * LANES] * Q_SCALE).astype(BF16)
        blk = q[:, n_nope + hh * LANES:n_nope + (hh + 1) * LANES]
        q_ref[0, hh, :, LANES:QK_PAD] = (_rope_block(blk, cos, sin) * Q_SCALE).astype(BF16)

    u = z[:, a2:]

    @pl.when(t == 0)
    def _():
        ext_ref[0:POOL_PAD, :] = prev_ref[0]

    ext_ref[POOL_PAD:POOL_PAD + tm, :] = u
    mixed = []
    for g, w in enumerate(POOL_WINDOWS):
        lo, hi = g * LANES, (g + 1) * LANES
        s = ext_ref[:, lo:hi]
        k = 1
        while k < w:
            s = s + pltpu.roll(s, k, 0)
            k *= 2
        inv_head = jnp.where(t == 0, inv_ref[:, lo:hi], 1.0 / w)
        head = s[POOL_PAD:2 * POOL_PAD] * inv_head - u[:POOL_PAD, lo:hi]
        body = s[2 * POOL_PAD:] * (1.0 / w) - u[POOL_PAD:, lo:hi]
        pooled = jnp.concatenate([head, body], axis=0).astype(BF16)
        mixed.append(jnp.dot(pooled, wpool_ref[g], preferred_element_type=F32))
    o_pool = jnp.concatenate(mixed, axis=1) * pscale_ref[...]
    opool_ref[0] = _rms(o_pool, gpool_ref[...]).astype(BF16)

    @pl.when(t == n_t - 1)
    def _():
        newpool_ref[0] = ext_ref[tm + 1:tm + POOL_PAD, :]

    ext_ref[0:POOL_PAD, :] = ext_ref[tm:tm + POOL_PAD, :]


def _pre_call(x, prev, inv_head, cos, sin, wts, *, tm):
    B, T, D = x.shape
    pool_w = prev.shape[-1]
    assert T % tm == 0 and T >= POOL_PAD and tm >= 2 * POOL_PAD and pool_w == len(POOL_WINDOWS) * LANES
    grid = (B, T // tm)
    in_specs = [
        pl.BlockSpec((1, tm, D), lambda b, t: (b, t, 0)),
        pl.BlockSpec((1, POOL_PAD, pool_w), lambda b, t: (b, 0, 0)),
        _const_spec(inv_head.shape),
        pl.BlockSpec((tm, LANES), lambda b, t: (t, 0)),
        pl.BlockSpec((tm, LANES), lambda b, t: (t, 0)),
    ] + [_const_spec(w.shape) for w in wts]
    out_shape = (
        jax.ShapeDtypeStruct((B, N_HEADS, T, QK_PAD), BF16),
        jax.ShapeDtypeStruct((B, T, QK_PAD), BF16),
        jax.ShapeDtypeStruct((B, T, KV_LORA_RANK), F32),
        jax.ShapeDtypeStruct((B, T, QK_ROPE_DIM), F32),
        jax.ShapeDtypeStruct((B, T, pool_w), BF16),
        jax.ShapeDtypeStruct((B, POOL_STATE, pool_w), F32),
    )
    out_specs = (
        pl.BlockSpec((1, N_HEADS, tm, QK_PAD), lambda b, t: (b, 0, t, 0)),
        pl.BlockSpec((1, tm, QK_PAD), lambda b, t: (b, t, 0)),
        pl.BlockSpec((1, tm, KV_LORA_RANK), lambda b, t: (b, t, 0)),
        pl.BlockSpec((1, tm, QK_ROPE_DIM), lambda b, t: (b, t, 0)),
        pl.BlockSpec((1, tm, pool_w), lambda b, t: (b, t, 0)),
        pl.BlockSpec((1, POOL_STATE, pool_w), lambda b, t: (b, 0, 0)),
    )
    return pl.pallas_call(
        functools.partial(_pre_kernel, tm=tm, pool_w=pool_w),
        grid=grid, in_specs=in_specs, out_specs=out_specs, out_shape=out_shape,
        scratch_shapes=[pltpu.VMEM((POOL_PAD + tm, pool_w), F32)],
        compiler_params=pltpu.CompilerParams(
            dimension_semantics=("arbitrary", "arbitrary"), vmem_limit_bytes=VMEM_LIMIT),
        name=f"pre_t{tm}",
    )(x, prev, inv_head, cos, sin, *wts)


def _attn_prompt_kernel(q_ref, k_ref, o_ref, s0_ref, s1_ref, mc0_ref, mc1_ref, p_ref, alpha_ref, m_ref, acc_ref,
                        *, tq, tk):
    qi = pl.program_id(1)
    rows = N_HEADS * tq
    nt = (((1,), (1,)), ((), ()))
    n_full = (qi * tq) // tk
    slabs = [slice(i * ATTN_SLAB, (i + 1) * ATTN_SLAB) for i in range(rows // ATTN_SLAB)]

    def lane_parts(s):
        return [s[:, c * LANES:(c + 1) * LANES] for c in range(tk // LANES)]

    def keys(j):
        return k_ref[0, pl.ds(pl.multiple_of(j * tk, tk), tk), :]

    def scores(j, buf, masked):
        s_ref, mc_ref = buf
        s_all = lax.dot_general(q_ref[0].reshape(rows, QK_PAD), keys(j), nt, preferred_element_type=F32)
        for i, r in enumerate(slabs):
            s = s_all[r, :]
            if masked:
                tok = qi * tq + (i * ATTN_SLAB) % tq + lax.broadcasted_iota(jnp.int32, s.shape, 0)
                key = j * tk + lax.broadcasted_iota(jnp.int32, s.shape, 1)
                s = jnp.where(key // CHUNK <= tok // CHUNK, s, -jnp.inf)
            s_ref[r, :] = s
            m_cur = jnp.max(functools.reduce(jnp.maximum, lane_parts(s)), axis=-1, keepdims=True)
            mc_ref[r, :] = jnp.broadcast_to(m_cur, (ATTN_SLAB, LANES))

    def softmax_pv(j, buf):
        s_ref, mc_ref = buf
        for r in slabs:
            m_prev = m_ref[r, :]
            m_new = jnp.maximum(m_prev, mc_ref[r, :])
            alpha_ref[r, :] = jnp.exp2(m_prev - m_new)
            m_ref[r, :] = m_new
            for c, part in enumerate(lane_parts(s_ref[r, :])):
                p_ref[r, c * LANES:(c + 1) * LANES] = jnp.exp2(part - m_new).astype(BF16)
        alpha = alpha_ref[...]
        pv = jnp.dot(p_ref[...], keys(j), preferred_element_type=F32)
        for c in range(QK_PAD // LANES):
            cols = slice(c * LANES, (c + 1) * LANES)
            acc_ref[:, cols] = alpha * acc_ref[:, cols] + pv[:, cols]

    m_ref[...] = jnp.full(m_ref.shape, -jnp.inf, F32)
    acc_ref[...] = jnp.zeros(acc_ref.shape, F32)

    bufs = ((s0_ref, mc0_ref), (s1_ref, mc1_ref))

    @pl.when(n_full == 0)
    def _():
        scores(0, bufs[0], True)
        softmax_pv(0, bufs[0])

    @pl.when(n_full > 0)
    def _():
        scores(0, bufs[0], False)

    def pair(i, carry):
        j = 2 * i
        scores(j + 1, bufs[1], False)
        softmax_pv(j, bufs[0])
        scores(j + 2, bufs[0], False)
        softmax_pv(j + 1, bufs[1])
        return carry

    n_pairs = jnp.maximum(n_full - 1, 0) // 2
    lax.fori_loop(0, n_pairs, pair, 0)

    @pl.when(n_full % 2 == 1)
    def _():
        scores(n_full, bufs[1], True)
        softmax_pv(n_full - 1, bufs[0])
        softmax_pv(n_full, bufs[1])

    @pl.when((n_full % 2 == 0) & (n_full > 0))
    def _():
        scores(n_full - 1, bufs[1], False)
        softmax_pv(n_full - 2, bufs[0])

    @pl.when((n_full % 2 == 0) & (n_full > 1))
    def _():
        scores(n_full, bufs[0], True)
        softmax_pv(n_full - 1, bufs[1])

    @pl.when(((n_full + 1) % 2 == 1) & (n_full > 0))
    def _():
        softmax_pv(n_full, bufs[0])

    l = acc_ref[:, QK_PAD - 1:QK_PAD]
    o = (acc_ref[:, 0:KV_LORA_RANK] / l).astype(BF16)
    for hh in range(N_HEADS):
        o_ref[0, :, hh * LANES:(hh + 1) * LANES] = o[hh * tq:(hh + 1) * tq]


def _attn_prompt_call(q, k, *, tq, tk):
    B, H, T, _ = q.shape
    assert T % tq == 0 and T % tk == 0 and tq % CHUNK == 0 and tq % ATTN_SLAB == 0 and tk % LANES == 0
    rows = H * tq
    return pl.pallas_call(
        functools.partial(_attn_prompt_kernel, tq=tq, tk=tk),
        grid=(B, T // tq),
        in_specs=[
            pl.BlockSpec((1, H, tq, QK_PAD), lambda b, i: (b, 0, i, 0)),
            pl.BlockSpec((1, T, QK_PAD), lambda b, i: (b, 0, 0)),
        ],
        out_specs=pl.BlockSpec((1, tq, H * KV_LORA_RANK), lambda b, i: (b, i, 0)),
        out_shape=jax.ShapeDtypeStruct((B, T, H * KV_LORA_RANK), BF16),
        scratch_shapes=[pltpu.VMEM((rows, tk), F32), pltpu.VMEM((rows, tk), F32),
                        pltpu.VMEM((rows, LANES), F32), pltpu.VMEM((rows, LANES), F32),
                        pltpu.VMEM((rows, tk), BF16),
                        pltpu.VMEM((rows, LANES), F32),
                        pltpu.VMEM((rows, LANES), F32),
                        pltpu.VMEM((rows, QK_PAD), F32)],
        compiler_params=pltpu.CompilerParams(
            dimension_semantics=("arbitrary", "arbitrary"), vmem_limit_bytes=VMEM_LIMIT),
        name="attn_prompt",
    )(q, k)


def _attn_sample_kernel(q_ref, ckv_ref, ckr_ref, knew_ref, o_ref, *, tq, past):
    rows = N_HEADS * tq
    q = q_ref[0].reshape(rows, QK_PAD)
    q_lat = q[:, :KV_LORA_RANK]
    q_rope = q[:, KV_LORA_RANK:KV_LORA_RANK + QK_ROPE_DIM]
    nt = (((1,), (1,)), ((), ()))
    v_past = ckv_ref[0].astype(BF16)
    s_past = (lax.dot_general(q_lat, v_past, nt, preferred_element_type=F32)
              + lax.dot_general(q_rope, ckr_ref[0].astype(BF16), nt, preferred_element_type=F32))
    k_new = knew_ref[0]
    s_new = lax.dot_general(q, k_new, nt, preferred_element_type=F32)

    def masked(s, k_start):
        q_pos = past + lax.broadcasted_iota(jnp.int32, s.shape, 0) % tq
        k_pos = k_start + lax.broadcasted_iota(jnp.int32, s.shape, 1)
        return jnp.where(k_pos // CHUNK <= q_pos // CHUNK, s, -jnp.inf)

    s_past = masked(s_past, 0)
    s_new = masked(s_new, past)
    m = jnp.maximum(jnp.max(s_past, axis=-1, keepdims=True), jnp.max(s_new, axis=-1, keepdims=True))
    p_past = jnp.exp2(s_past - m)
    p_new = jnp.exp2(s_new - m)
    l = jnp.sum(p_past, axis=-1, keepdims=True) + jnp.sum(p_new, axis=-1, keepdims=True)
    acc = (jnp.dot(p_past.astype(BF16), v_past, preferred_element_type=F32)
           + jnp.dot(p_new.astype(BF16), k_new[:, :KV_LORA_RANK], preferred_element_type=F32))
    o = (acc / l).astype(BF16)
    for hh in range(N_HEADS):
        o_ref[0, :, hh * LANES:(hh + 1) * LANES] = o[hh * tq:(hh + 1) * tq]


def _attn_sample_call(q, cache_kv, cache_kr, k_new):
    B, H, tq, _ = q.shape
    past = cache_kv.shape[1]
    return pl.pallas_call(
        functools.partial(_attn_sample_kernel, tq=tq, past=past),
        grid=(B,),
        in_specs=[
            pl.BlockSpec((1, H, tq, QK_PAD), lambda b: (b, 0, 0, 0)),
            pl.BlockSpec((1, past, KV_LORA_RANK), lambda b: (b, 0, 0)),
            pl.BlockSpec((1, past, QK_ROPE_DIM), lambda b: (b, 0, 0)),
            pl.BlockSpec((1, tq, QK_PAD), lambda b: (b, 0, 0)),
        ],
        out_specs=pl.BlockSpec((1, tq, H * KV_LORA_RANK), lambda b: (b, 0, 0)),
        out_shape=jax.ShapeDtypeStruct((B, tq, H * KV_LORA_RANK), BF16),
        compiler_params=pltpu.CompilerParams(
            dimension_semantics=("arbitrary",), vmem_limit_bytes=VMEM_LIMIT),
        name="attn_sample",
    )(q, cache_kv, cache_kr, k_new)


def _post_kernel(x_ref, olat_ref, opool_ref, wuv_ref, gattn_ref, wo_ref, g2_ref, wg_ref, wu_ref, wd_ref,
                 gf_ref, y_ref):
    o_attn = jnp.dot(olat_ref[0], wuv_ref[...], preferred_element_type=F32)
    a_n = _rms(o_attn, gattn_ref[...]).astype(BF16)
    merged = jnp.concatenate([a_n, opool_ref[0]], axis=1)
    x1 = x_ref[0] + jnp.dot(merged, wo_ref[...], preferred_element_type=F32)
    h2 = _rms(x1, g2_ref[...]).astype(BF16)
    gate = jnp.dot(h2, wg_ref[...], preferred_element_type=F32)
    up = jnp.dot(h2, wu_ref[...], preferred_element_type=F32)
    act = (gate * jax.nn.sigmoid(gate) * up).astype(BF16)
    x2 = x1 + jnp.dot(act, wd_ref[...], preferred_element_type=F32)
    y_ref[0] = _rms(x2, gf_ref[...])


def _post_call(x, o_lat, o_pool, wts, *, tm):
    B, T, D = x.shape
    assert T % tm == 0
    in_specs = [
        pl.BlockSpec((1, tm, D), lambda b, t: (b, t, 0)),
        pl.BlockSpec((1, tm, o_lat.shape[-1]), lambda b, t: (b, t, 0)),
        pl.BlockSpec((1, tm, o_pool.shape[-1]), lambda b, t: (b, t, 0)),
    ] + [_const_spec(w.shape) for w in wts]
    return pl.pallas_call(
        _post_kernel,
        grid=(B, T // tm),
        in_specs=in_specs,
        out_specs=pl.BlockSpec((1, tm, D), lambda b, t: (b, t, 0)),
        out_shape=jax.ShapeDtypeStruct((B, T, D), F32),
        compiler_params=pltpu.CompilerParams(
            dimension_semantics=("arbitrary", "arbitrary"), vmem_limit_bytes=VMEM_LIMIT),
        name=f"post_t{tm}",
    )(x, o_lat, o_pool, *wts)


def _swap_halves(w):
    half = w.shape[-1] // 2
    return jnp.concatenate([-w[..., half:], w[..., :half]], axis=-1)


def _rope_cols(w):
    pad = jnp.zeros(w.shape[:-1] + (LANES - 2 * QK_ROPE_DIM,), w.dtype)
    return jnp.concatenate([w, _swap_halves(w), pad], axis=-1)


def _rope_tables(pos):
    d = QK_ROPE_DIM
    freqs = jnp.power(ROPE_BASE, -jnp.arange(0, d, 2, dtype=F32) / d)
    ang = pos.astype(F32)[:, None] * freqs[None, :]
    pad = jnp.zeros((pos.shape[0], LANES - d), F32)
    cos = jnp.concatenate([jnp.cos(ang), jnp.cos(ang), pad], axis=-1)
    sin = jnp.concatenate([jnp.sin(ang), jnp.sin(ang), pad], axis=-1)
    return cos, sin


def _pool_inv_head(pos0, pool_w):
    pos = pos0 + jnp.arange(POOL_PAD)
    win = jnp.repeat(jnp.asarray(POOL_WINDOWS, jnp.int32), pool_w // len(POOL_WINDOWS))
    return 1.0 / jnp.minimum(pos[:, None] + 1, win[None, :]).astype(F32)


def _layer(x, pos0, prev, attn_fn, pre_w, post_w, *, tm_pre, tm_post):
    T = x.shape[1]
    cos, sin = _rope_tables(pos0 + jnp.arange(T))
    inv_head = _pool_inv_head(pos0, prev.shape[-1])
    q, k, ckv, kr, o_pool, new_pool = _pre_call(x, prev, inv_head, cos, sin, pre_w, tm=tm_pre)
    o_lat = attn_fn(q, k)
    y = _post_call(x, o_lat, o_pool, post_w, tm=tm_post)
    return y, ckv, kr, new_pool


def kernel(x_prompt, x_sample, cache_kv_latent, cache_k_rope, state_pool, g_norm1, w_in, g_q, w_uq, g_kv,
           w_uk, w_uv, w_pool, pool_scale, g_out_attn, g_out_pool, w_o, g_norm2, w_gate, w_up, w_down,
           g_final):
    depth = w_in.shape[0]
    assert depth == 1
    l = 0
    pool_w = w_pool.shape[1] * w_pool.shape[2]
    a0 = Q_LORA_RANK
    a1 = a0 + KV_LORA_RANK
    a2 = a1 + QK_ROPE_DIM
    win = w_in[l]
    w1 = jnp.concatenate([win[:, :a1], _rope_cols(win[:, a1:a2]), win[:, a2:]], axis=1).astype(BF16)
    wq = w_uq[l].reshape(Q_LORA_RANK, N_HEADS, QK_NOPE_DIM + QK_ROPE_DIM)
    w2 = jnp.concatenate([wq[..., :QK_NOPE_DIM].reshape(Q_LORA_RANK, -1),
                          _rope_cols(wq[..., QK_NOPE_DIM:]).reshape(Q_LORA_RANK, -1)], axis=1).astype(BF16)
    eye = jnp.eye(N_HEADS, dtype=F32)
    w3 = jnp.einsum('hrd,hg->hdgr', w_uk[l], eye).reshape(N_HEADS * QK_NOPE_DIM, N_HEADS * KV_LORA_RANK)
    wuv = jnp.einsum('hrd,hg->hrgd', w_uv[l], eye).reshape(N_HEADS * KV_LORA_RANK, -1)
    row = lambda g: g.reshape(1, -1).astype(F32)
    pre_w = (row(g_norm1[l]), w1, row(g_q[l]), w2, w3.astype(BF16), row(g_kv[l]), w_pool[l].astype(BF16),
             row(pool_scale[l]), row(g_out_pool[l]))
    post_w = (wuv.astype(BF16), row(g_out_attn[l]), w_o[l].astype(BF16), row(g_norm2[l]),
              w_gate[l].astype(BF16), w_up[l].astype(BF16), w_down[l].astype(BF16), row(g_final))

    B, T, _ = x_prompt.shape
    prev_p = jnp.zeros((B, POOL_PAD, pool_w), F32)
    y_p, ckv_p, kr_p, pool_p = _layer(
        x_prompt, 0, prev_p, functools.partial(_attn_prompt_call, tq=256, tk=512), pre_w, post_w,
        tm_pre=512, tm_post=256)

    past = cache_kv_latent.shape[2]
    Ts = x_sample.shape[1]
    prev_s = jnp.pad(state_pool[l], ((0, 0), (POOL_PAD - POOL_STATE, 0), (0, 0)))
    attn_s = lambda q, k: _attn_sample_call(q, cache_kv_latent[l], cache_k_rope[l], k)
    y_s, ckv_s, kr_s, pool_s = _layer(x_sample, past, prev_s, attn_s, pre_w, post_w, tm_pre=Ts, tm_post=Ts)

    return (y_p, y_s, ckv_p[None], kr_p[None], pool_p[None], ckv_s[None], kr_s[None], pool_s[None])
```

```python
import functools
import math

import jax
import jax.numpy as jnp
from jax import lax
from jax.experimental import pallas as pl
from jax.experimental.pallas import tpu as pltpu

CHUNK = 64
N_HEADS = 8
QK_NOPE_DIM = 64
QK_ROPE_DIM = 32
Q_LORA_RANK = 256
KV_LORA_RANK = 128
ROPE_BASE = 10000.0
POOL_WINDOWS = (2, 4, 8, 16)
POOL_STATE = max(POOL_WINDOWS) - 1
POOL_PAD = POOL_STATE + 1
SM_SCALE = 1.0 / math.sqrt(QK_NOPE_DIM + QK_ROPE_DIM)
Q_SCALE = SM_SCALE * math.log2(math.e)
EPS = 1e-6
ATTN_SLAB = 64
FFN_CHUNK = 1024
LANES = 128
QK_PAD = 2 * LANES
ROPE_ROLL = LANES - QK_ROPE_DIM
VMEM_LIMIT = 56 * 1024 * 1024

F32 = jnp.float32
BF16 = jnp.bfloat16


def _rms(x, g):
    ms = jnp.mean(x * x, axis=-1, keepdims=True)
    return x * lax.rsqrt(ms + EPS) * g


def _rope_block(blk, cos, sin):
    return blk * cos + pltpu.roll(blk, ROPE_ROLL, 1) * sin


def _const_spec(shape):
    nd = len(shape)
    return pl.BlockSpec(shape, lambda *_: (0,) * nd, pipeline_mode=pl.Buffered(1))


def _pre_kernel(x_ref, prev_ref, inv_ref, cos_ref, sin_ref, g1_ref, w1_ref, gq_ref, w2_ref, w3_ref,
                gkv_ref, wpool_ref, pscale_ref, gpool_ref,
                q_ref, k_ref, ckv_ref, kr_ref, opool_ref, newpool_ref, ext_ref, *, tm, pool_w):
    t = pl.program_id(1)
    n_t = pl.num_programs(1)
    x = x_ref[0]
    h = _rms(x, g1_ref[...]).astype(BF16)
    z = jnp.dot(h, w1_ref[...], preferred_element_type=F32)
    a0 = Q_LORA_RANK
    a1 = a0 + KV_LORA_RANK
    a2 = a1 + LANES
    cos = cos_ref[...]
    sin = sin_ref[...]

    ckv_n = _rms(z[:, a0:a1], gkv_ref[...])
    k_rot = _rope_block(z[:, a1:a2], cos, sin)
    ckv_ref[0] = ckv_n
    kr_ref[0] = k_rot[:, :QK_ROPE_DIM]
    k_ref[0, :, 0:LANES] = ckv_n.astype(BF16)
    spare = lax.broadcasted_iota(jnp.int32, k_rot.shape, 1) >= QK_ROPE_DIM
    k_ref[0, :, LANES:QK_PAD] = jnp.where(spare, 1.0, k_rot).astype(BF16)

    cqn = _rms(z[:, :a0], gq_ref[...]).astype(BF16)
    q = jnp.dot(cqn, w2_ref[...], preferred_element_type=F32)
    n_nope = N_HEADS * QK_NOPE_DIM
    q_lat = jnp.dot(q[:, :n_nope].astype(BF16), w3_ref[...], preferred_element_type=F32)
    for hh in range(N_HEADS):
        q_ref[0, hh, :, 0:LANES] = (q_lat[:, hh * LANES:(hh + 1) * LANES] * Q_SCALE).astype(BF16)
        blk = q[:, n_nope + hh * LANES:n_nope + (hh + 1) * LANES]
        q_ref[0, hh, :, LANES:QK_PAD] = (_rope_block(blk, cos, sin) * Q_SCALE).astype(BF16)

    u = z[:, a2:]

    @pl.when(t == 0)
    def _():
        ext_ref[0:POOL_PAD, :] = prev_ref[0]

    ext_ref[POOL_PAD:POOL_PAD + tm, :] = u
    mixed = []
    for g, w in enumerate(POOL_WINDOWS):
        lo, hi = g * LANES, (g + 1) * LANES
        s = ext_ref[:, lo:hi]
        k = 1
        while k < w:
            s = s + pltpu.roll(s, k, 0)
            k *= 2
        inv_head = jnp.where(t == 0, inv_ref[:, lo:hi], 1.0 / w)
        head = s[POOL_PAD:2 * POOL_PAD] * inv_head - u[:POOL_PAD, lo:hi]
        body = s[2 * POOL_PAD:] * (1.0 / w) - u[POOL_PAD:, lo:hi]
        pooled = jnp.concatenate([head, body], axis=0).astype(BF16)
        mixed.append(jnp.dot(pooled, wpool_ref[g], preferred_element_type=F32))
    o_pool = jnp.concatenate(mixed, axis=1) * pscale_ref[...]
    opool_ref[0] = _rms(o_pool, gpool_ref[...]).astype(BF16)

    @pl.when(t == n_t - 1)
    def _():
        newpool_ref[0] = ext_ref[tm + 1:tm + POOL_PAD, :]

    ext_ref[0:POOL_PAD, :] = ext_ref[tm:tm + POOL_PAD, :]


def _pre_call(x, prev, inv_head, cos, sin, wts, *, tm):
    B, T, D = x.shape
    pool_w = prev.shape[-1]
    assert T % tm == 0 and T >= POOL_PAD and tm >= 2 * POOL_PAD and pool_w == len(POOL_WINDOWS) * LANES
    grid = (B, T // tm)
    in_specs = [
        pl.BlockSpec((1, tm, D), lambda b, t: (b, t, 0)),
        pl.BlockSpec((1, POOL_PAD, pool_w), lambda b, t: (b, 0, 0)),
        _const_spec(inv_head.shape),
        pl.BlockSpec((tm, LANES), lambda b, t: (t, 0)),
        pl.BlockSpec((tm, LANES), lambda b, t: (t, 0)),
    ] + [_const_spec(w.shape) for w in wts]
    out_shape = (
        jax.ShapeDtypeStruct((B, N_HEADS, T, QK_PAD), BF16),
        jax.ShapeDtypeStruct((B, T, QK_PAD), BF16),
        jax.ShapeDtypeStruct((B, T, KV_LORA_RANK), F32),
        jax.ShapeDtypeStruct((B, T, QK_ROPE_DIM), F32),
        jax.ShapeDtypeStruct((B, T, pool_w), BF16),
        jax.ShapeDtypeStruct((B, POOL_STATE, pool_w), F32),
    )
    out_specs = (
        pl.BlockSpec((1, N_HEADS, tm, QK_PAD), lambda b, t: (b, 0, t, 0)),
        pl.BlockSpec((1, tm, QK_PAD), lambda b, t: (b, t, 0)),
        pl.BlockSpec((1, tm, KV_LORA_RANK), lambda b, t: (b, t, 0)),
        pl.BlockSpec((1, tm, QK_ROPE_DIM), lambda b, t: (b, t, 0)),
        pl.BlockSpec((1, tm, pool_w), lambda b, t: (b, t, 0)),
        pl.BlockSpec((1, POOL_STATE, pool_w), lambda b, t: (b, 0, 0)),
    )
    return pl.pallas_call(
        functools.partial(_pre_kernel, tm=tm, pool_w=pool_w),
        grid=grid, in_specs=in_specs, out_specs=out_specs, out_shape=out_shape,
        scratch_shapes=[pltpu.VMEM((POOL_PAD + tm, pool_w), F32)],
        compiler_params=pltpu.CompilerParams(
            dimension_semantics=("arbitrary", "arbitrary"), vmem_limit_bytes=VMEM_LIMIT),
        name=f"pre_t{tm}",
    )(x, prev, inv_head, cos, sin, *wts)


def _attn_prompt_kernel(q_ref, k_ref, o_ref, s0_ref, s1_ref, mc0_ref, mc1_ref, p_ref, alpha_ref, m_ref, acc_ref,
                        *, tq, tk):
    qi = pl.program_id(1)
    rows = N_HEADS * tq
    nt = (((1,), (1,)), ((), ()))
    n_full = (qi * tq) // tk
    slabs = [slice(i * ATTN_SLAB, (i + 1) * ATTN_SLAB) for i in range(rows // ATTN_SLAB)]

    def lane_parts(s):
        return [s[:, c * LANES:(c + 1) * LANES] for c in range(tk // LANES)]

    def keys(j):
        return k_ref[0, pl.ds(pl.multiple_of(j * tk, tk), tk), :]

    def scores(j, buf, masked):
        s_ref, mc_ref = buf
        s_all = lax.dot_general(q_ref[0].reshape(rows, QK_PAD), keys(j), nt, preferred_element_type=F32)
        for i, r in enumerate(slabs):
            s = s_all[r, :]
            if masked:
                tok = qi * tq + (i * ATTN_SLAB) % tq + lax.broadcasted_iota(jnp.int32, s.shape, 0)
                key = j * tk + lax.broadcasted_iota(jnp.int32, s.shape, 1)
                s = jnp.where(key // CHUNK <= tok // CHUNK, s, -jnp.inf)
            s_ref[r, :] = s
            m_cur = jnp.max(functools.reduce(jnp.maximum, lane_parts(s)), axis=-1, keepdims=True)
            mc_ref[r, :] = jnp.broadcast_to(m_cur, (ATTN_SLAB, LANES))

    def softmax_pv(j, buf):
        s_ref, mc_ref = buf
        for r in slabs:
            m_prev = m_ref[r, :]
            m_new = jnp.maximum(m_prev, mc_ref[r, :])
            alpha_ref[r, :] = jnp.exp2(m_prev - m_new)
            m_ref[r, :] = m_new
            for c, part in enumerate(lane_parts(s_ref[r, :])):
                p_ref[r, c * LANES:(c + 1) * LANES] = jnp.exp2(part - m_new).astype(BF16)
        alpha = alpha_ref[...]
        pv = jnp.dot(p_ref[...], keys(j), preferred_element_type=F32)
        for c in range(QK_PAD // LANES):
            cols = slice(c * LANES, (c + 1) * LANES)
            acc_ref[:, cols] = alpha * acc_ref[:, cols] + pv[:, cols]

    m_ref[...] = jnp.full(m_ref.shape, -jnp.inf, F32)
    acc_ref[...] = jnp.zeros(acc_ref.shape, F32)

    bufs = ((s0_ref, mc0_ref), (s1_ref, mc1_ref))

    @pl.when(n_full == 0)
    def _():
        scores(0, bufs[0], True)
        softmax_pv(0, bufs[0])

    @pl.when(n_full > 0)
    def _():
        scores(0, bufs[0], False)

    def pair(i, carry):
        j = 2 * i
        scores(j + 1, bufs[1], False)
        softmax_pv(j, bufs[0])
        scores(j + 2, bufs[0], False)
        softmax_pv(j + 1, bufs[1])
        return carry

    n_pairs = jnp.maximum(n_full - 1, 0) // 2
    lax.fori_loop(0, n_pairs, pair, 0)

    @pl.when(n_full % 2 == 1)
    def _():
        scores(n_full, bufs[1], True)
        softmax_pv(n_full - 1, bufs[0])
        softmax_pv(n_full, bufs[1])

    @pl.when((n_full % 2 == 0) & (n_full > 0))
    def _():
        scores(n_full - 1, bufs[1], False)
        softmax_pv(n_full - 2, bufs[0])

    @pl.when((n_full % 2 == 0) & (n_full > 1))
    def _():
        scores(n_full, bufs[0], True)
        softmax_pv(n_full - 1, bufs[1])

    @pl.when(((n_full + 1) % 2 == 1) & (n_full > 0))
    def _():
        softmax_pv(n_full, bufs[0])

    l = acc_ref[:, QK_PAD - 1:QK_PAD]
    o = (acc_ref[:, 0:KV_LORA_RANK] / l).astype(BF16)
    for hh in range(N_HEADS):
        o_ref[0, :, hh * LANES:(hh + 1) * LANES] = o[hh * tq:(hh + 1) * tq]


def _attn_prompt_call(q, k, *, tq, tk):
    B, H, T, _ = q.shape
    assert T % tq == 0 and T % tk == 0 and tq % CHUNK == 0 and tq % ATTN_SLAB == 0 and tk % LANES == 0
    rows = H * tq
    return pl.pallas_call(
        functools.partial(_attn_prompt_kernel, tq=tq, tk=tk),
        grid=(B, T // tq),
        in_specs=[
            pl.BlockSpec((1, H, tq, QK_PAD), lambda b, i: (b, 0, i, 0)),
            pl.BlockSpec((1, T, QK_PAD), lambda b, i: (b, 0, 0)),
        ],
        out_specs=pl.BlockSpec((1, tq, H * KV_LORA_RANK), lambda b, i: (b, i, 0)),
        out_shape=jax.ShapeDtypeStruct((B, T, H * KV_LORA_RANK), BF16),
        scratch_shapes=[pltpu.VMEM((rows, tk), F32), pltpu.VMEM((rows, tk), F32),
                        pltpu.VMEM((rows, LANES), F32), pltpu.VMEM((rows, LANES), F32),
                        pltpu.VMEM((rows, tk), BF16),
                        pltpu.VMEM((rows, LANES), F32),
                        pltpu.VMEM((rows, LANES), F32),
                        pltpu.VMEM((rows, QK_PAD), F32)],
        compiler_params=pltpu.CompilerParams(
            dimension_semantics=("arbitrary", "arbitrary"), vmem_limit_bytes=VMEM_LIMIT),
        name="attn_prompt",
    )(q, k)


def _attn_sample_kernel(q_ref, ckv_ref, ckr_ref, knew_ref, o_ref, *, tq, past):
    rows = N_HEADS * tq
    q = q_ref[0].reshape(rows, QK_PAD)
    q_lat = q[:, :KV_LORA_RANK]
    q_rope = q[:, KV_LORA_RANK:KV_LORA_RANK + QK_ROPE_DIM]
    nt = (((1,), (1,)), ((), ()))
    v_past = ckv_ref[0].astype(BF16)
    s_past = (lax.dot_general(q_lat, v_past, nt, preferred_element_type=F32)
              + lax.dot_general(q_rope, ckr_ref[0].astype(BF16), nt, preferred_element_type=F32))
    k_new = knew_ref[0]
    s_new = lax.dot_general(q, k_new, nt, preferred_element_type=F32)

    def masked(s, k_start):
        q_pos = past + lax.broadcasted_iota(jnp.int32, s.shape, 0) % tq
        k_pos = k_start + lax.broadcasted_iota(jnp.int32, s.shape, 1)
        return jnp.where(k_pos // CHUNK <= q_pos // CHUNK, s, -jnp.inf)

    s_past = masked(s_past, 0)
    s_new = masked(s_new, past)
    m = jnp.maximum(jnp.max(s_past, axis=-1, keepdims=True), jnp.max(s_new, axis=-1, keepdims=True))
    p_past = jnp.exp2(s_past - m)
    p_new = jnp.exp2(s_new - m)
    l = jnp.sum(p_past, axis=-1, keepdims=True) + jnp.sum(p_new, axis=-1, keepdims=True)
    acc = (jnp.dot(p_past.astype(BF16), v_past, preferred_element_type=F32)
           + jnp.dot(p_new.astype(BF16), k_new[:, :KV_LORA_RANK], preferred_element_type=F32))
    o = (acc / l).astype(BF16)
    for hh in range(N_HEADS):
        o_ref[0, :, hh * LANES:(hh + 1) * LANES] = o[hh * tq:(hh + 1) * tq]


def _attn_sample_call(q, cache_kv, cache_kr, k_new):
    B, H, tq, _ = q.shape
    past = cache_kv.shape[1]
    return pl.pallas_call(
        functools.partial(_attn_sample_kernel, tq=tq, past=past),
        grid=(B,),
        in_specs=[
            pl.BlockSpec((1, H, tq, QK_PAD), lambda b: (b, 0, 0, 0)),
            pl.BlockSpec((1, past, KV_LORA_RANK), lambda b: (b, 0, 0)),
            pl.BlockSpec((1, past, QK_ROPE_DIM), lambda b: (b, 0, 0)),
            pl.BlockSpec((1, tq, QK_PAD), lambda b: (b, 0, 0)),
        ],
        out_specs=pl.BlockSpec((1, tq, H * KV_LORA_RANK), lambda b: (b, 0, 0)),
        out_shape=jax.ShapeDtypeStruct((B, tq, H * KV_LORA_RANK), BF16),
        compiler_params=pltpu.CompilerParams(
            dimension_semantics=("arbitrary",), vmem_limit_bytes=VMEM_LIMIT),
        name="attn_sample",
    )(q, cache_kv, cache_kr, k_new)


def _post_kernel(x_ref, olat_ref, opool_ref, wuv_ref, gattn_ref, wo_ref, g2_ref, wg_ref, wu_ref, wd_ref,
                 gf_ref, y_ref):
    o_attn = jnp.dot(olat_ref[0], wuv_ref[...], preferred_element_type=F32)
    a_n = _rms(o_attn, gattn_ref[...]).astype(BF16)
    merged = jnp.concatenate([a_n, opool_ref[0]], axis=1)
    x1 = x_ref[0] + jnp.dot(merged, wo_ref[...], preferred_element_type=F32)
    h2 = _rms(x1, g2_ref[...]).astype(BF16)
    d_ff = wg_ref.shape[1]
    x2 = x1
    for lo in range(0, d_ff, FFN_CHUNK):
        hi = min(lo + FFN_CHUNK, d_ff)
        gate = jnp.dot(h2, wg_ref[:, lo:hi], preferred_element_type=F32)
        up = jnp.dot(h2, wu_ref[:, lo:hi], preferred_element_type=F32)
        act = (gate * jax.nn.sigmoid(gate) * up).astype(BF16)
        x2 = x2 + jnp.dot(act, wd_ref[lo:hi, :], preferred_element_type=F32)
    y_ref[0] = _rms(x2, gf_ref[...])


def _post_call(x, o_lat, o_pool, wts, *, tm):
    B, T, D = x.shape
    assert T % tm == 0
    in_specs = [
        pl.BlockSpec((1, tm, D), lambda b, t: (b, t, 0)),
        pl.BlockSpec((1, tm, o_lat.shape[-1]), lambda b, t: (b, t, 0)),
        pl.BlockSpec((1, tm, o_pool.shape[-1]), lambda b, t: (b, t, 0)),
    ] + [_const_spec(w.shape) for w in wts]
    return pl.pallas_call(
        _post_kernel,
        grid=(B, T // tm),
        in_specs=in_specs,
        out_specs=pl.BlockSpec((1, tm, D), lambda b, t: (b, t, 0)),
        out_shape=jax.ShapeDtypeStruct((B, T, D), F32),
        compiler_params=pltpu.CompilerParams(
            dimension_semantics=("arbitrary", "arbitrary"), vmem_limit_bytes=VMEM_LIMIT),
        name=f"post_t{tm}",
    )(x, o_lat, o_pool, *wts)


def _swap_halves(w):
    half = w.shape[-1] // 2
    return jnp.concatenate([-w[..., half:], w[..., :half]], axis=-1)


def _rope_cols(w):
    pad = jnp.zeros(w.shape[:-1] + (LANES - 2 * QK_ROPE_DIM,), w.dtype)
    return jnp.concatenate([w, _swap_halves(w), pad], axis=-1)


def _rope_tables(pos):
    d = QK_ROPE_DIM
    freqs = jnp.power(ROPE_BASE, -jnp.arange(0, d, 2, dtype=F32) / d)
    ang = pos.astype(F32)[:, None] * freqs[None, :]
    pad = jnp.zeros((pos.shape[0], LANES - d), F32)
    cos = jnp.concatenate([jnp.cos(ang), jnp.cos(ang), pad], axis=-1)
    sin = jnp.concatenate([jnp.sin(ang), jnp.sin(ang), pad], axis=-1)
    return cos, sin


def _pool_inv_head(pos0, pool_w):
    pos = pos0 + jnp.arange(POOL_PAD)
    win = jnp.repeat(jnp.asarray(POOL_WINDOWS, jnp.int32), pool_w // len(POOL_WINDOWS))
    return 1.0 / jnp.minimum(pos[:, None] + 1, win[None, :]).astype(F32)


def _layer(x, pos0, prev, attn_fn, pre_w, post_w, *, tm_pre, tm_post):
    T = x.shape[1]
    cos, sin = _rope_tables(pos0 + jnp.arange(T))
    inv_head = _pool_inv_head(pos0, prev.shape[-1])
    q, k, ckv, kr, o_pool, new_pool = _pre_call(x, prev, inv_head, cos, sin, pre_w, tm=tm_pre)
    o_lat = attn_fn(q, k)
    y = _post_call(x, o_lat, o_pool, post_w, tm=tm_post)
    return y, ckv, kr, new_pool


def kernel(x_prompt, x_sample, cache_kv_latent, cache_k_rope, state_pool, g_norm1, w_in, g_q, w_uq, g_kv,
           w_uk, w_uv, w_pool, pool_scale, g_out_attn, g_out_pool, w_o, g_norm2, w_gate, w_up, w_down,
           g_final):
    depth = w_in.shape[0]
    assert depth == 1
    l = 0
    pool_w = w_pool.shape[1] * w_pool.shape[2]
    a0 = Q_LORA_RANK
    a1 = a0 + KV_LORA_RANK
    a2 = a1 + QK_ROPE_DIM
    win = w_in[l]
    w1 = jnp.concatenate([win[:, :a1], _rope_cols(win[:, a1:a2]), win[:, a2:]], axis=1).astype(BF16)
    wq = w_uq[l].reshape(Q_LORA_RANK, N_HEADS, QK_NOPE_DIM + QK_ROPE_DIM)
    w2 = jnp.concatenate([wq[..., :QK_NOPE_DIM].reshape(Q_LORA_RANK, -1),
                          _rope_cols(wq[..., QK_NOPE_DIM:]).reshape(Q_LORA_RANK, -1)], axis=1).astype(BF16)
    eye = jnp.eye(N_HEADS, dtype=F32)
    w3 = jnp.einsum('hrd,hg->hdgr', w_uk[l], eye).reshape(N_HEADS * QK_NOPE_DIM, N_HEADS * KV_LORA_RANK)
    wuv = jnp.einsum('hrd,hg->hrgd', w_uv[l], eye).reshape(N_HEADS * KV_LORA_RANK, -1)
    row = lambda g: g.reshape(1, -1).astype(F32)
    pre_w = (row(g_norm1[l]), w1, row(g_q[l]), w2, w3.astype(BF16), row(g_kv[l]), w_pool[l].astype(BF16),
             row(pool_scale[l]), row(g_out_pool[l]))
    post_w = (wuv.astype(BF16), row(g_out_attn[l]), w_o[l].astype(BF16), row(g_norm2[l]),
              w_gate[l].astype(BF16), w_up[l].astype(BF16), w_down[l].astype(BF16), row(g_final))

    B, T, _ = x_prompt.shape
    prev_p = jnp.zeros((B, POOL_PAD, pool_w), F32)
    y_p, ckv_p, kr_p, pool_p = _layer(
        x_prompt, 0, prev_p, functools.partial(_attn_prompt_call, tq=256, tk=512), pre_w, post_w,
        tm_pre=512, tm_post=512)

    past = cache_kv_latent.shape[2]
    Ts = x_sample.shape[1]
    prev_s = jnp.pad(state_pool[l], ((0, 0), (POOL_PAD - POOL_STATE, 0), (0, 0)))
    attn_s = lambda q, k: _attn_sample_call(q, cache_kv_latent[l], cache_k_rope[l], k)
    y_s, ckv_s, kr_s, pool_s = _layer(x_sample, past, prev_s, attn_s, pre_w, post_w, tm_pre=Ts, tm_post=Ts)

    return (y_p, y_s, ckv_p[None], kr_p[None], pool_p[None], ckv_s[None], kr_s[None], pool_s[None])
```

```python
import functools
import math

import jax
import jax.numpy as jnp
from jax import lax
from jax.experimental import pallas as pl
from jax.experimental.pallas import tpu as pltpu

CHUNK = 64
N_HEADS = 8
QK_NOPE_DIM = 64
QK_ROPE_DIM = 32
Q_LORA_RANK = 256
KV_LORA_RANK = 128
ROPE_BASE = 10000.0
POOL_WINDOWS = (2, 4, 8, 16)
POOL_STATE = max(POOL_WINDOWS) - 1
POOL_PAD = POOL_STATE + 1
SM_SCALE = 1.0 / math.sqrt(QK_NOPE_DIM + QK_ROPE_DIM)
Q_SCALE = SM_SCALE * math.log2(math.e)
EPS = 1e-6
ATTN_SLAB = 64
FFN_CHUNK = 1024
LANES = 128
QK_PAD = 2 * LANES
ROPE_ROLL = LANES - QK_ROPE_DIM
VMEM_LIMIT = 56 * 1024 * 1024

F32 = jnp.float32
BF16 = jnp.bfloat16


def _rms(x, g):
    ms = jnp.mean(x * x, axis=-1, keepdims=True)
    return x * lax.rsqrt(ms + EPS) * g


def _rope_block(blk, cos, sin):
    return blk * cos + pltpu.roll(blk, ROPE_ROLL, 1) * sin


def _const_spec(shape):
    nd = len(shape)
    return pl.BlockSpec(shape, lambda *_: (0,) * nd, pipeline_mode=pl.Buffered(1))


def _pre_kernel(x_ref, prev_ref, inv_ref, cos_ref, sin_ref, g1_ref, w1_ref, gq_ref, w2_ref, w3_ref,
                gkv_ref, wpool_ref, pscale_ref, gpool_ref,
                q_ref, k_ref, ckv_ref, kr_ref, opool_ref, newpool_ref, ext_ref, *, tm, pool_w):
    t = pl.program_id(1)
    n_t = pl.num_programs(1)
    x = x_ref[0]
    h = _rms(x, g1_ref[...]).astype(BF16)
    z = jnp.dot(h, w1_ref[...], preferred_element_type=F32)
    a0 = Q_LORA_RANK
    a1 = a0 + KV_LORA_RANK
    a2 = a1 + LANES
    cos = cos_ref[...]
    sin = sin_ref[...]

    ckv_n = _rms(z[:, a0:a1], gkv_ref[...])
    k_rot = _rope_block(z[:, a1:a2], cos, sin)
    ckv_ref[0] = ckv_n
    kr_ref[0] = k_rot[:, :QK_ROPE_DIM]
    k_ref[0, :, 0:LANES] = ckv_n.astype(BF16)
    spare = lax.broadcasted_iota(jnp.int32, k_rot.shape, 1) >= QK_ROPE_DIM
    k_ref[0, :, LANES:QK_PAD] = jnp.where(spare, 1.0, k_rot).astype(BF16)

    cqn = _rms(z[:, :a0], gq_ref[...]).astype(BF16)
    q = jnp.dot(cqn, w2_ref[...], preferred_element_type=F32)
    n_nope = N_HEADS * QK_NOPE_DIM
    q_lat = jnp.dot(q[:, :n_nope].astype(BF16), w3_ref[...], preferred_element_type=F32)
    for hh in range(N_HEADS):
        q_ref[0, hh, :, 0:LANES] = (q_lat[:, hh * LANES:(hh + 1) * LANES] * Q_SCALE).astype(BF16)
        blk = q[:, n_nope + hh * LANES:n_nope + (hh + 1) * LANES]
        q_ref[0, hh, :, LANES:QK_PAD] = (_rope_block(blk, cos, sin) * Q_SCALE).astype(BF16)

    u = z[:, a2:]

    @pl.when(t == 0)
    def _():
        ext_ref[0:POOL_PAD, :] = prev_ref[0]

    ext_ref[POOL_PAD:POOL_PAD + tm, :] = u
    mixed = []
    for g, w in enumerate(POOL_WINDOWS):
        lo, hi = g * LANES, (g + 1) * LANES
        s = ext_ref[:, lo:hi]
        k = 1
        while k < w:
            s = s + pltpu.roll(s, k, 0)
            k *= 2
        inv_head = jnp.where(t == 0, inv_ref[:, lo:hi], 1.0 / w)
        head = s[POOL_PAD:2 * POOL_PAD] * inv_head - u[:POOL_PAD, lo:hi]
        body = s[2 * POOL_PAD:] * (1.0 / w) - u[POOL_PAD:, lo:hi]
        pooled = jnp.concatenate([head, body], axis=0).astype(BF16)
        mixed.append(jnp.dot(pooled, wpool_ref[g], preferred_element_type=F32))
    o_pool = jnp.concatenate(mixed, axis=1) * pscale_ref[...]
    opool_ref[0] = _rms(o_pool, gpool_ref[...]).astype(BF16)

    @pl.when(t == n_t - 1)
    def _():
        newpool_ref[0] = ext_ref[tm + 1:tm + POOL_PAD, :]

    ext_ref[0:POOL_PAD, :] = ext_ref[tm:tm + POOL_PAD, :]


def _pre_call(x, prev, inv_head, cos, sin, wts, *, tm):
    B, T, D = x.shape
    pool_w = prev.shape[-1]
    assert T % tm == 0 and T >= POOL_PAD and tm >= 2 * POOL_PAD and pool_w == len(POOL_WINDOWS) * LANES
    grid = (B, T // tm)
    in_specs = [
        pl.BlockSpec((1, tm, D), lambda b, t: (b, t, 0)),
        pl.BlockSpec((1, POOL_PAD, pool_w), lambda b, t: (b, 0, 0)),
        _const_spec(inv_head.shape),
        pl.BlockSpec((tm, LANES), lambda b, t: (t, 0)),
        pl.BlockSpec((tm, LANES), lambda b, t: (t, 0)),
    ] + [_const_spec(w.shape) for w in wts]
    out_shape = (
        jax.ShapeDtypeStruct((B, N_HEADS, T, QK_PAD), BF16),
        jax.ShapeDtypeStruct((B, T, QK_PAD), BF16),
        jax.ShapeDtypeStruct((B, T, KV_LORA_RANK), F32),
        jax.ShapeDtypeStruct((B, T, QK_ROPE_DIM), F32),
        jax.ShapeDtypeStruct((B, T, pool_w), BF16),
        jax.ShapeDtypeStruct((B, POOL_STATE, pool_w), F32),
    )
    out_specs = (
        pl.BlockSpec((1, N_HEADS, tm, QK_PAD), lambda b, t: (b, 0, t, 0)),
        pl.BlockSpec((1, tm, QK_PAD), lambda b, t: (b, t, 0)),
        pl.BlockSpec((1, tm, KV_LORA_RANK), lambda b, t: (b, t, 0)),
        pl.BlockSpec((1, tm, QK_ROPE_DIM), lambda b, t: (b, t, 0)),
        pl.BlockSpec((1, tm, pool_w), lambda b, t: (b, t, 0)),
        pl.BlockSpec((1, POOL_STATE, pool_w), lambda b, t: (b, 0, 0)),
    )
    return pl.pallas_call(
        functools.partial(_pre_kernel, tm=tm, pool_w=pool_w),
        grid=grid, in_specs=in_specs, out_specs=out_specs, out_shape=out_shape,
        scratch_shapes=[pltpu.VMEM((POOL_PAD + tm, pool_w), F32)],
        compiler_params=pltpu.CompilerParams(
            dimension_semantics=("arbitrary", "arbitrary"), vmem_limit_bytes=VMEM_LIMIT),
        name=f"pre_t{tm}",
    )(x, prev, inv_head, cos, sin, *wts)


def _attn_prompt_kernel(q_ref, k_ref, o_ref, s0_ref, s1_ref, mc0_ref, mc1_ref, p_ref, alpha_ref, m_ref, acc_ref,
                        *, tq, tk):
    qi = pl.program_id(1)
    rows = N_HEADS * tq
    nt = (((1,), (1,)), ((), ()))
    n_full = (qi * tq) // tk
    slabs = [slice(i * ATTN_SLAB, (i + 1) * ATTN_SLAB) for i in range(rows // ATTN_SLAB)]

    def lane_parts(s):
        return [s[:, c * LANES:(c + 1) * LANES] for c in range(s.shape[1] // LANES)]

    def keys(j, width):
        return k_ref[0, pl.ds(pl.multiple_of(j * tk, tk), width), :]

    def scores(j, buf, masked, width=tk):
        s_ref, mc_ref = buf
        s_all = lax.dot_general(q_ref[0].reshape(rows, QK_PAD), keys(j, width), nt,
                                preferred_element_type=F32)
        for i, r in enumerate(slabs):
            s = s_all[r, :]
            if masked:
                tok = qi * tq + (i * ATTN_SLAB) % tq + lax.broadcasted_iota(jnp.int32, s.shape, 0)
                key = j * tk + lax.broadcasted_iota(jnp.int32, s.shape, 1)
                s = jnp.where(key // CHUNK <= tok // CHUNK, s, -jnp.inf)
            s_ref[r, 0:width] = s
            m_cur = jnp.max(functools.reduce(jnp.maximum, lane_parts(s)), axis=-1, keepdims=True)
            mc_ref[r, :] = jnp.broadcast_to(m_cur, (ATTN_SLAB, LANES))

    def softmax_pv(j, buf, width=tk):
        s_ref, mc_ref = buf
        for r in slabs:
            m_prev = m_ref[r, :]
            m_new = jnp.maximum(m_prev, mc_ref[r, :])
            alpha_ref[r, :] = jnp.exp2(m_prev - m_new)
            m_ref[r, :] = m_new
            for c, part in enumerate(lane_parts(s_ref[r, 0:width])):
                p_ref[r, c * LANES:(c + 1) * LANES] = jnp.exp2(part - m_new).astype(BF16)
        alpha = alpha_ref[...]
        pv = jnp.dot(p_ref[:, 0:width], keys(j, width), preferred_element_type=F32)
        for c in range(QK_PAD // LANES):
            cols = slice(c * LANES, (c + 1) * LANES)
            acc_ref[:, cols] = alpha * acc_ref[:, cols] + pv[:, cols]

    m_ref[...] = jnp.full(m_ref.shape, -jnp.inf, F32)
    acc_ref[...] = jnp.zeros(acc_ref.shape, F32)

    bufs = ((s0_ref, mc0_ref), (s1_ref, mc1_ref))
    tail = (qi + 1) * tq - n_full * tk
    tail_widths = range(tq, tk + 1, tq)

    for w in tail_widths:
        @pl.when((n_full == 0) & (tail == w))
        def _():
            scores(0, bufs[0], True, w)
            softmax_pv(0, bufs[0], w)

    @pl.when(n_full > 0)
    def _():
        scores(0, bufs[0], False)

    def pair(i, carry):
        j = 2 * i
        scores(j + 1, bufs[1], False)
        softmax_pv(j, bufs[0])
        scores(j + 2, bufs[0], False)
        softmax_pv(j + 1, bufs[1])
        return carry

    n_pairs = jnp.maximum(n_full - 1, 0) // 2
    lax.fori_loop(0, n_pairs, pair, 0)

    for w in tail_widths:
        @pl.when((n_full % 2 == 1) & (tail == w))
        def _():
            scores(n_full, bufs[1], True, w)
            softmax_pv(n_full - 1, bufs[0])
            softmax_pv(n_full, bufs[1], w)

    @pl.when((n_full % 2 == 0) & (n_full > 0))
    def _():
        scores(n_full - 1, bufs[1], False)
        softmax_pv(n_full - 2, bufs[0])

    for w in tail_widths:
        @pl.when((n_full % 2 == 0) & (n_full > 1) & (tail == w))
        def _():
            scores(n_full, bufs[0], True, w)
            softmax_pv(n_full - 1, bufs[1])

        @pl.when(((n_full + 1) % 2 == 1) & (n_full > 0) & (tail == w))
        def _():
            softmax_pv(n_full, bufs[0], w)

    l = acc_ref[:, QK_PAD - 1:QK_PAD]
    o = (acc_ref[:, 0:KV_LORA_RANK] / l).astype(BF16)
    for hh in range(N_HEADS):
        o_ref[0, :, hh * LANES:(hh + 1) * LANES] = o[hh * tq:(hh + 1) * tq]


def _attn_prompt_call(q, k, *, tq, tk):
    B, H, T, _ = q.shape
    assert T % tq == 0 and T % tk == 0 and tq % CHUNK == 0 and tq % ATTN_SLAB == 0 and tk % LANES == 0
    rows = H * tq
    return pl.pallas_call(
        functools.partial(_attn_prompt_kernel, tq=tq, tk=tk),
        grid=(B, T // tq),
        in_specs=[
            pl.BlockSpec((1, H, tq, QK_PAD), lambda b, i: (b, 0, i, 0)),
            pl.BlockSpec((1, T, QK_PAD), lambda b, i: (b, 0, 0)),
        ],
        out_specs=pl.BlockSpec((1, tq, H * KV_LORA_RANK), lambda b, i: (b, i, 0)),
        out_shape=jax.ShapeDtypeStruct((B, T, H * KV_LORA_RANK), BF16),
        scratch_shapes=[pltpu.VMEM((rows, tk), F32), pltpu.VMEM((rows, tk), F32),
                        pltpu.VMEM((rows, LANES), F32), pltpu.VMEM((rows, LANES), F32),
                        pltpu.VMEM((rows, tk), BF16),
                        pltpu.VMEM((rows, LANES), F32),
                        pltpu.VMEM((rows, LANES), F32),
                        pltpu.VMEM((rows, QK_PAD), F32)],
        compiler_params=pltpu.CompilerParams(
            dimension_semantics=("arbitrary", "arbitrary"), vmem_limit_bytes=VMEM_LIMIT),
        name="attn_prompt",
    )(q, k)


def _attn_sample_kernel(q_ref, ckv_ref, ckr_ref, knew_ref, o_ref, *, tq, past):
    rows = N_HEADS * tq
    q = q_ref[0].reshape(rows, QK_PAD)
    q_lat = q[:, :KV_LORA_RANK]
    q_rope = q[:, KV_LORA_RANK:KV_LORA_RANK + QK_ROPE_DIM]
    nt = (((1,), (1,)), ((), ()))
    v_past = ckv_ref[0].astype(BF16)
    s_past = (lax.dot_general(q_lat, v_past, nt, preferred_element_type=F32)
              + lax.dot_general(q_rope, ckr_ref[0].astype(BF16), nt, preferred_element_type=F32))
    k_new = knew_ref[0]
    s_new = lax.dot_general(q, k_new, nt, preferred_element_type=F32)

    def masked(s, k_start):
        q_pos = past + lax.broadcasted_iota(jnp.int32, s.shape, 0) % tq
        k_pos = k_start + lax.broadcasted_iota(jnp.int32, s.shape, 1)
        return jnp.where(k_pos // CHUNK <= q_pos // CHUNK, s, -jnp.inf)

    s_past = masked(s_past, 0)
    s_new = masked(s_new, past)
    m = jnp.maximum(jnp.max(s_past, axis=-1, keepdims=True), jnp.max(s_new, axis=-1, keepdims=True))
    p_past = jnp.exp2(s_past - m)
    p_new = jnp.exp2(s_new - m)
    l = jnp.sum(p_past, axis=-1, keepdims=True) + jnp.sum(p_new, axis=-1, keepdims=True)
    acc = (jnp.dot(p_past.astype(BF16), v_past, preferred_element_type=F32)
           + jnp.dot(p_new.astype(BF16), k_new[:, :KV_LORA_RANK], preferred_element_type=F32))
    o = (acc / l).astype(BF16)
    for hh in range(N_HEADS):
        o_ref[0, :, hh * LANES:(hh + 1) * LANES] = o[hh * tq:(hh + 1) * tq]


def _attn_sample_call(q, cache_kv, cache_kr, k_new):
    B, H, tq, _ = q.shape
    past = cache_kv.shape[1]
    return pl.pallas_call(
        functools.partial(_attn_sample_kernel, tq=tq, past=past),
        grid=(B,),
        in_specs=[
            pl.BlockSpec((1, H, tq, QK_PAD), lambda b: (b, 0, 0, 0)),
            pl.BlockSpec((1, past, KV_LORA_RANK), lambda b: (b, 0, 0)),
            pl.BlockSpec((1, past, QK_ROPE_DIM), lambda b: (b, 0, 0)),
            pl.BlockSpec((1, tq, QK_PAD), lambda b: (b, 0, 0)),
        ],
        out_specs=pl.BlockSpec((1, tq, H * KV_LORA_RANK), lambda b: (b, 0, 0)),
        out_shape=jax.ShapeDtypeStruct((B, tq, H * KV_LORA_RANK), BF16),
        compiler_params=pltpu.CompilerParams(
            dimension_semantics=("arbitrary",), vmem_limit_bytes=VMEM_LIMIT),
        name="attn_sample",
    )(q, cache_kv, cache_kr, k_new)


def _post_kernel(x_ref, olat_ref, opool_ref, wuv_ref, gattn_ref, wo_ref, g2_ref, wg_ref, wu_ref, wd_ref,
                 gf_ref, y_ref):
    pair_w = 2 * KV_LORA_RANK
    o_attn = jnp.concatenate(
        [jnp.dot(olat_ref[0, :, i * pair_w:(i + 1) * pair_w], wuv_ref[i], preferred_element_type=F32)
         for i in range(wuv_ref.shape[0])], axis=1)
    a_n = _rms(o_attn, gattn_ref[...]).astype(BF16)
    merged = jnp.concatenate([a_n, opool_ref[0]], axis=1)
    x1 = x_ref[0] + jnp.dot(merged, wo_ref[...], preferred_element_type=F32)
    h2 = _rms(x1, g2_ref[...]).astype(BF16)
    d_ff = wg_ref.shape[1]
    x2 = x1
    for lo in range(0, d_ff, FFN_CHUNK):
        hi = min(lo + FFN_CHUNK, d_ff)
        gate = jnp.dot(h2, wg_ref[:, lo:hi], preferred_element_type=F32)
        up = jnp.dot(h2, wu_ref[:, lo:hi], preferred_element_type=F32)
        act = (gate * jax.nn.sigmoid(gate) * up).astype(BF16)
        x2 = x2 + jnp.dot(act, wd_ref[lo:hi, :], preferred_element_type=F32)
    y_ref[0] = _rms(x2, gf_ref[...])


def _post_call(x, o_lat, o_pool, wts, *, tm):
    B, T, D = x.shape
    assert T % tm == 0
    in_specs = [
        pl.BlockSpec((1, tm, D), lambda b, t: (b, t, 0)),
        pl.BlockSpec((1, tm, o_lat.shape[-1]), lambda b, t: (b, t, 0)),
        pl.BlockSpec((1, tm, o_pool.shape[-1]), lambda b, t: (b, t, 0)),
    ] + [_const_spec(w.shape) for w in wts]
    return pl.pallas_call(
        _post_kernel,
        grid=(B, T // tm),
        in_specs=in_specs,
        out_specs=pl.BlockSpec((1, tm, D), lambda b, t: (b, t, 0)),
        out_shape=jax.ShapeDtypeStruct((B, T, D), F32),
        compiler_params=pltpu.CompilerParams(
            dimension_semantics=("arbitrary", "arbitrary"), vmem_limit_bytes=VMEM_LIMIT),
        name=f"post_t{tm}",
    )(x, o_lat, o_pool, *wts)


def _swap_halves(w):
    half = w.shape[-1] // 2
    return jnp.concatenate([-w[..., half:], w[..., :half]], axis=-1)


def _rope_cols(w):
    pad = jnp.zeros(w.shape[:-1] + (LANES - 2 * QK_ROPE_DIM,), w.dtype)
    return jnp.concatenate([w, _swap_halves(w), pad], axis=-1)


def _rope_tables(pos):
    d = QK_ROPE_DIM
    freqs = jnp.power(ROPE_BASE, -jnp.arange(0, d, 2, dtype=F32) / d)
    ang = pos.astype(F32)[:, None] * freqs[None, :]
    pad = jnp.zeros((pos.shape[0], LANES - d), F32)
    cos = jnp.concatenate([jnp.cos(ang), jnp.cos(ang), pad], axis=-1)
    sin = jnp.concatenate([jnp.sin(ang), jnp.sin(ang), pad], axis=-1)
    return cos, sin


def _pool_inv_head(pos0, pool_w):
    pos = pos0 + jnp.arange(POOL_PAD)
    win = jnp.repeat(jnp.asarray(POOL_WINDOWS, jnp.int32), pool_w // len(POOL_WINDOWS))
    return 1.0 / jnp.minimum(pos[:, None] + 1, win[None, :]).astype(F32)


def _layer(x, pos0, prev, attn_fn, pre_w, post_w, *, tm_pre, tm_post):
    T = x.shape[1]
    cos, sin = _rope_tables(pos0 + jnp.arange(T))
    inv_head = _pool_inv_head(pos0, prev.shape[-1])
    q, k, ckv, kr, o_pool, new_pool = _pre_call(x, prev, inv_head, cos, sin, pre_w, tm=tm_pre)
    o_lat = attn_fn(q, k)
    B, _, D = x.shape
    flat = lambda a: a.reshape(1, B * T, a.shape[-1])
    y = _post_call(flat(x), flat(o_lat), flat(o_pool), post_w, tm=tm_post).reshape(B, T, D)
    return y, ckv, kr, new_pool


def kernel(x_prompt, x_sample, cache_kv_latent, cache_k_rope, state_pool, g_norm1, w_in, g_q, w_uq, g_kv,
           w_uk, w_uv, w_pool, pool_scale, g_out_attn, g_out_pool, w_o, g_norm2, w_gate, w_up, w_down,
           g_final):
    depth = w_in.shape[0]
    assert depth == 1
    l = 0
    pool_w = w_pool.shape[1] * w_pool.shape[2]
    a0 = Q_LORA_RANK
    a1 = a0 + KV_LORA_RANK
    a2 = a1 + QK_ROPE_DIM
    win = w_in[l]
    w1 = jnp.concatenate([win[:, :a1], _rope_cols(win[:, a1:a2]), win[:, a2:]], axis=1).astype(BF16)
    wq = w_uq[l].reshape(Q_LORA_RANK, N_HEADS, QK_NOPE_DIM + QK_ROPE_DIM)
    w2 = jnp.concatenate([wq[..., :QK_NOPE_DIM].reshape(Q_LORA_RANK, -1),
                          _rope_cols(wq[..., QK_NOPE_DIM:]).reshape(Q_LORA_RANK, -1)], axis=1).astype(BF16)
    eye = jnp.eye(N_HEADS, dtype=F32)
    w3 = jnp.einsum('hrd,hg->hdgr', w_uk[l], eye).reshape(N_HEADS * QK_NOPE_DIM, N_HEADS * KV_LORA_RANK)
    wuv = jnp.einsum('phrd,hg->phrgd', w_uv[l].reshape(N_HEADS // 2, 2, KV_LORA_RANK, -1),
                     jnp.eye(2, dtype=F32)).reshape(N_HEADS // 2, 2 * KV_LORA_RANK, -1)
    row = lambda g: g.reshape(1, -1).astype(F32)
    pre_w = (row(g_norm1[l]), w1, row(g_q[l]), w2, w3.astype(BF16), row(g_kv[l]), w_pool[l].astype(BF16),
             row(pool_scale[l]), row(g_out_pool[l]))
    post_w = (wuv.astype(BF16), row(g_out_attn[l]), w_o[l].astype(BF16), row(g_norm2[l]),
              w_gate[l].astype(BF16), w_up[l].astype(BF16), w_down[l].astype(BF16), row(g_final))

    B, T, _ = x_prompt.shape
    prev_p = jnp.zeros((B, POOL_PAD, pool_w), F32)
    y_p, ckv_p, kr_p, pool_p = _layer(
        x_prompt, 0, prev_p, functools.partial(_attn_prompt_call, tq=256, tk=512), pre_w, post_w,
        tm_pre=512, tm_post=512)

    past = cache_kv_latent.shape[2]
    Ts = x_sample.shape[1]
    prev_s = jnp.pad(state_pool[l], ((0, 0), (POOL_PAD - POOL_STATE, 0), (0, 0)))
    attn_s = lambda q, k: _attn_sample_call(q, cache_kv_latent[l], cache_k_rope[l], k)
    y_s, ckv_s, kr_s, pool_s = _layer(x_sample, past, prev_s, attn_s, pre_w, post_w, tm_pre=Ts,
                                      tm_post=x_sample.shape[0] * Ts)

    return (y_p, y_s, ckv_p[None], kr_p[None], pool_p[None], ckv_s[None], kr_s[None], pool_s[None])
```

```python
import functools
import math

import jax
import jax.numpy as jnp
from jax import lax
from jax.experimental import pallas as pl
from jax.experimental.pallas import tpu as pltpu

CHUNK = 64
N_HEADS = 8
QK_NOPE_DIM = 64
QK_ROPE_DIM = 32
Q_LORA_RANK = 256
KV_LORA_RANK = 128
ROPE_BASE = 10000.0
POOL_WINDOWS = (2, 4, 8, 16)
POOL_STATE = max(POOL_WINDOWS) - 1
POOL_PAD = POOL_STATE + 1
SM_SCALE = 1.0 / math.sqrt(QK_NOPE_DIM + QK_ROPE_DIM)
Q_SCALE = SM_SCALE * math.log2(math.e)
EPS = 1e-6
ATTN_SLAB = 64
FFN_CHUNK = 1024
LANES = 128
QK_PAD = 2 * LANES
ROPE_ROLL = LANES - QK_ROPE_DIM
VMEM_LIMIT = 56 * 1024 * 1024

F32 = jnp.float32
BF16 = jnp.bfloat16


def _rms(x, g):
    ms = jnp.mean(x * x, axis=-1, keepdims=True)
    return x * lax.rsqrt(ms + EPS) * g


def _rope_block(blk, cos, sin):
    return blk * cos + pltpu.roll(blk, ROPE_ROLL, 1) * sin


def _const_spec(shape):
    nd = len(shape)
    return pl.BlockSpec(shape, lambda *_: (0,) * nd, pipeline_mode=pl.Buffered(1))


def _pre_kernel(x_ref, prev_ref, inv_ref, cos_ref, sin_ref, g1_ref, w1_ref, gq_ref, w2_ref, w3_ref,
                gkv_ref, wpool_ref, pscale_ref, gpool_ref,
                q_ref, k_ref, ckv_ref, kr_ref, opool_ref, newpool_ref, ext_ref, *, tm, pool_w):
    t = pl.program_id(1)
    n_t = pl.num_programs(1)
    x = x_ref[0]
    h = _rms(x, g1_ref[...]).astype(BF16)
    z = jnp.dot(h, w1_ref[...], preferred_element_type=F32)
    a0 = Q_LORA_RANK
    a1 = a0 + KV_LORA_RANK
    a2 = a1 + LANES
    cos = cos_ref[...]
    sin = sin_ref[...]

    ckv_n = _rms(z[:, a0:a1], gkv_ref[...])
    k_rot = _rope_block(z[:, a1:a2], cos, sin)
    ckv_ref[0] = ckv_n
    kr_ref[0] = k_rot[:, :QK_ROPE_DIM]
    k_ref[0, :, 0:LANES] = ckv_n.astype(BF16)
    spare = lax.broadcasted_iota(jnp.int32, k_rot.shape, 1) >= QK_ROPE_DIM
    k_ref[0, :, LANES:QK_PAD] = jnp.where(spare, 1.0, k_rot).astype(BF16)

    cqn = _rms(z[:, :a0], gq_ref[...]).astype(BF16)
    q = jnp.dot(cqn, w2_ref[...], preferred_element_type=F32)
    n_nope = N_HEADS * QK_NOPE_DIM
    q_lat = jnp.dot(q[:, :n_nope].astype(BF16), w3_ref[...], preferred_element_type=F32)
    for hh in range(N_HEADS):
        q_ref[0, hh, :, 0:LANES] = (q_lat[:, hh * LANES:(hh + 1) * LANES] * Q_SCALE).astype(BF16)
        blk = q[:, n_nope + hh * LANES:n_nope + (hh + 1) * LANES]
        q_ref[0, hh, :, LANES:QK_PAD] = (_rope_block(blk, cos, sin) * Q_SCALE).astype(BF16)

    u = z[:, a2:]

    @pl.when(t == 0)
    def _():
        ext_ref[0:POOL_PAD, :] = prev_ref[0]

    ext_ref[POOL_PAD:POOL_PAD + tm, :] = u
    mixed = []
    for g, w in enumerate(POOL_WINDOWS):
        lo, hi = g * LANES, (g + 1) * LANES
        s = ext_ref[:, lo:hi]
        k = 1
        while k < w:
            s = s + pltpu.roll(s, k, 0)
            k *= 2
        inv_head = jnp.where(t == 0, inv_ref[:, lo:hi], 1.0 / w)
        head = s[POOL_PAD:2 * POOL_PAD] * inv_head - u[:POOL_PAD, lo:hi]
        body = s[2 * POOL_PAD:] * (1.0 / w) - u[POOL_PAD:, lo:hi]
        pooled = jnp.concatenate([head, body], axis=0).astype(BF16)
        mixed.append(jnp.dot(pooled, wpool_ref[g], preferred_element_type=F32))
    o_pool = jnp.concatenate(mixed, axis=1) * pscale_ref[...]
    opool_ref[0] = _rms(o_pool, gpool_ref[...]).astype(BF16)

    @pl.when(t == n_t - 1)
    def _():
        newpool_ref[0] = ext_ref[tm + 1:tm + POOL_PAD, :]

    ext_ref[0:POOL_PAD, :] = ext_ref[tm:tm + POOL_PAD, :]


def _pre_call(x, prev, inv_head, cos, sin, wts, *, tm):
    B, T, D = x.shape
    pool_w = prev.shape[-1]
    assert T % tm == 0 and T >= POOL_PAD and tm >= 2 * POOL_PAD and pool_w == len(POOL_WINDOWS) * LANES
    grid = (B, T // tm)
    in_specs = [
        pl.BlockSpec((1, tm, D), lambda b, t: (b, t, 0)),
        pl.BlockSpec((1, POOL_PAD, pool_w), lambda b, t: (b, 0, 0)),
        _const_spec(inv_head.shape),
        pl.BlockSpec((tm, LANES), lambda b, t: (t, 0)),
        pl.BlockSpec((tm, LANES), lambda b, t: (t, 0)),
    ] + [_const_spec(w.shape) for w in wts]
    out_shape = (
        jax.ShapeDtypeStruct((B, N_HEADS, T, QK_PAD), BF16),
        jax.ShapeDtypeStruct((B, T, QK_PAD), BF16),
        jax.ShapeDtypeStruct((B, T, KV_LORA_RANK), F32),
        jax.ShapeDtypeStruct((B, T, QK_ROPE_DIM), F32),
        jax.ShapeDtypeStruct((B, T, pool_w), BF16),
        jax.ShapeDtypeStruct((B, POOL_STATE, pool_w), F32),
    )
    out_specs = (
        pl.BlockSpec((1, N_HEADS, tm, QK_PAD), lambda b, t: (b, 0, t, 0)),
        pl.BlockSpec((1, tm, QK_PAD), lambda b, t: (b, t, 0)),
        pl.BlockSpec((1, tm, KV_LORA_RANK), lambda b, t: (b, t, 0)),
        pl.BlockSpec((1, tm, QK_ROPE_DIM), lambda b, t: (b, t, 0)),
        pl.BlockSpec((1, tm, pool_w), lambda b, t: (b, t, 0)),
        pl.BlockSpec((1, POOL_STATE, pool_w), lambda b, t: (b, 0, 0)),
    )
    return pl.pallas_call(
        functools.partial(_pre_kernel, tm=tm, pool_w=pool_w),
        grid=grid, in_specs=in_specs, out_specs=out_specs, out_shape=out_shape,
        scratch_shapes=[pltpu.VMEM((POOL_PAD + tm, pool_w), F32)],
        compiler_params=pltpu.CompilerParams(
            dimension_semantics=("arbitrary", "arbitrary"), vmem_limit_bytes=VMEM_LIMIT),
        name=f"pre_t{tm}",
    )(x, prev, inv_head, cos, sin, *wts)


def _attn_prompt_kernel(q_ref, k_ref, o_ref, s0_ref, s1_ref, mc0_ref, mc1_ref, p_ref, alpha_ref, m_ref, acc_ref,
                        *, tq, tk):
    qi = pl.program_id(1)
    rows = N_HEADS * tq
    nt = (((1,), (1,)), ((), ()))
    n_full = (qi * tq) // tk
    slabs = [slice(i * ATTN_SLAB, (i + 1) * ATTN_SLAB) for i in range(rows // ATTN_SLAB)]

    def lane_parts(s):
        return [s[:, c * LANES:(c + 1) * LANES] for c in range(s.shape[1] // LANES)]

    def keys(j, width):
        return k_ref[0, pl.ds(pl.multiple_of(j * tk, tk), width), :]

    def scores(j, buf, masked, width=tk):
        s_ref, mc_ref = buf
        s_all = lax.dot_general(q_ref[0].reshape(rows, QK_PAD), keys(j, width), nt,
                                preferred_element_type=F32)
        for i, r in enumerate(slabs):
            s = s_all[r, :]
            if masked:
                tok = qi * tq + (i * ATTN_SLAB) % tq + lax.broadcasted_iota(jnp.int32, s.shape, 0)
                key = j * tk + lax.broadcasted_iota(jnp.int32, s.shape, 1)
                s = jnp.where(key // CHUNK <= tok // CHUNK, s, -jnp.inf)
            s_ref[r, 0:width] = s
            m_cur = jnp.max(functools.reduce(jnp.maximum, lane_parts(s)), axis=-1, keepdims=True)
            mc_ref[r, :] = jnp.broadcast_to(m_cur, (ATTN_SLAB, LANES))

    def softmax_pv(j, buf, width=tk):
        softmax(buf, width)
        pv(j, width)

    def softmax(buf, width=tk):
        s_ref, mc_ref = buf
        for r in slabs:
            m_prev = m_ref[r, :]
            m_new = jnp.maximum(m_prev, mc_ref[r, :])
            alpha_ref[r, :] = jnp.exp2(m_prev - m_new)
            m_ref[r, :] = m_new
            for c, part in enumerate(lane_parts(s_ref[r, 0:width])):
                p_ref[r, c * LANES:(c + 1) * LANES] = jnp.exp2(part - m_new).astype(BF16)

    def pv(j, width=tk):
        alpha = alpha_ref[...]
        pv = jnp.dot(p_ref[:, 0:width], keys(j, width), preferred_element_type=F32)
        for c in range(QK_PAD // LANES):
            cols = slice(c * LANES, (c + 1) * LANES)
            acc_ref[:, cols] = alpha * acc_ref[:, cols] + pv[:, cols]

    m_ref[...] = jnp.full(m_ref.shape, -jnp.inf, F32)
    acc_ref[...] = jnp.zeros(acc_ref.shape, F32)

    bufs = ((s0_ref, mc0_ref), (s1_ref, mc1_ref))
    tail = (qi + 1) * tq - n_full * tk
    tail_widths = range(tq, tk + 1, tq)

    for w in tail_widths:
        @pl.when((n_full == 0) & (tail == w))
        def _():
            scores(0, bufs[0], True, w)
            softmax_pv(0, bufs[0], w)

    @pl.when(n_full > 0)
    def _():
        scores(0, bufs[0], False)

    def pair(i, carry):
        j = 2 * i
        scores(j + 1, bufs[1], False)
        softmax(bufs[0])
        scores(j + 2, bufs[0], False)
        pv(j)
        softmax_pv(j + 1, bufs[1])
        return carry

    n_pairs = jnp.maximum(n_full - 1, 0) // 2
    lax.fori_loop(0, n_pairs, pair, 0)

    for w in tail_widths:
        @pl.when((n_full % 2 == 1) & (tail == w))
        def _():
            scores(n_full, bufs[1], True, w)
            softmax_pv(n_full - 1, bufs[0])
            softmax_pv(n_full, bufs[1], w)

    @pl.when((n_full % 2 == 0) & (n_full > 0))
    def _():
        scores(n_full - 1, bufs[1], False)
        softmax_pv(n_full - 2, bufs[0])

    for w in tail_widths:
        @pl.when((n_full % 2 == 0) & (n_full > 1) & (tail == w))
        def _():
            scores(n_full, bufs[0], True, w)
            softmax_pv(n_full - 1, bufs[1])

        @pl.when(((n_full + 1) % 2 == 1) & (n_full > 0) & (tail == w))
        def _():
            softmax_pv(n_full, bufs[0], w)

    l = acc_ref[:, QK_PAD - 1:QK_PAD]
    o = (acc_ref[:, 0:KV_LORA_RANK] / l).astype(BF16)
    for hh in range(N_HEADS):
        o_ref[0, :, hh * LANES:(hh + 1) * LANES] = o[hh * tq:(hh + 1) * tq]


def _attn_prompt_call(q, k, *, tq, tk):
    B, H, T, _ = q.shape
    assert T % tq == 0 and T % tk == 0 and tq % CHUNK == 0 and tq % ATTN_SLAB == 0 and tk % LANES == 0
    rows = H * tq
    return pl.pallas_call(
        functools.partial(_attn_prompt_kernel, tq=tq, tk=tk),
        grid=(B, T // tq),
        in_specs=[
            pl.BlockSpec((1, H, tq, QK_PAD), lambda b, i: (b, 0, i, 0)),
            pl.BlockSpec((1, T, QK_PAD), lambda b, i: (b, 0, 0)),
        ],
        out_specs=pl.BlockSpec((1, tq, H * KV_LORA_RANK), lambda b, i: (b, i, 0)),
        out_shape=jax.ShapeDtypeStruct((B, T, H * KV_LORA_RANK), BF16),
        scratch_shapes=[pltpu.VMEM((rows, tk), F32), pltpu.VMEM((rows, tk), F32),
                        pltpu.VMEM((rows, LANES), F32), pltpu.VMEM((rows, LANES), F32),
                        pltpu.VMEM((rows, tk), BF16),
                        pltpu.VMEM((rows, LANES), F32),
                        pltpu.VMEM((rows, LANES), F32),
                        pltpu.VMEM((rows, QK_PAD), F32)],
        compiler_params=pltpu.CompilerParams(
            dimension_semantics=("arbitrary", "arbitrary"), vmem_limit_bytes=VMEM_LIMIT),
        name="attn_prompt",
    )(q, k)


def _attn_sample_kernel(q_ref, ckv_ref, ckr_ref, knew_ref, o_ref, *, tq, past):
    rows = N_HEADS * tq
    q = q_ref[0].reshape(rows, QK_PAD)
    q_lat = q[:, :KV_LORA_RANK]
    q_rope = q[:, KV_LORA_RANK:KV_LORA_RANK + QK_ROPE_DIM]
    nt = (((1,), (1,)), ((), ()))
    v_past = ckv_ref[0].astype(BF16)
    s_past = (lax.dot_general(q_lat, v_past, nt, preferred_element_type=F32)
              + lax.dot_general(q_rope, ckr_ref[0].astype(BF16), nt, preferred_element_type=F32))
    k_new = knew_ref[0]
    s_new = lax.dot_general(q, k_new, nt, preferred_element_type=F32)

    def masked(s, k_start):
        q_pos = past + lax.broadcasted_iota(jnp.int32, s.shape, 0) % tq
        k_pos = k_start + lax.broadcasted_iota(jnp.int32, s.shape, 1)
        return jnp.where(k_pos // CHUNK <= q_pos // CHUNK, s, -jnp.inf)

    s_past = masked(s_past, 0)
    s_new = masked(s_new, past)
    m = jnp.maximum(jnp.max(s_past, axis=-1, keepdims=True), jnp.max(s_new, axis=-1, keepdims=True))
    p_past = jnp.exp2(s_past - m)
    p_new = jnp.exp2(s_new - m)
    l = jnp.sum(p_past, axis=-1, keepdims=True) + jnp.sum(p_new, axis=-1, keepdims=True)
    acc = (jnp.dot(p_past.astype(BF16), v_past, preferred_element_type=F32)
           + jnp.dot(p_new.astype(BF16), k_new[:, :KV_LORA_RANK], preferred_element_type=F32))
    o = (acc / l).astype(BF16)
    for hh in range(N_HEADS):
        o_ref[0, :, hh * LANES:(hh + 1) * LANES] = o[hh * tq:(hh + 1) * tq]


def _attn_sample_call(q, cache_kv, cache_kr, k_new):
    B, H, tq, _ = q.shape
    past = cache_kv.shape[1]
    return pl.pallas_call(
        functools.partial(_attn_sample_kernel, tq=tq, past=past),
        grid=(B,),
        in_specs=[
            pl.BlockSpec((1, H, tq, QK_PAD), lambda b: (b, 0, 0, 0)),
            pl.BlockSpec((1, past, KV_LORA_RANK), lambda b: (b, 0, 0)),
            pl.BlockSpec((1, past, QK_ROPE_DIM), lambda b: (b, 0, 0)),
            pl.BlockSpec((1, tq, QK_PAD), lambda b: (b, 0, 0)),
        ],
        out_specs=pl.BlockSpec((1, tq, H * KV_LORA_RANK), lambda b: (b, 0, 0)),
        out_shape=jax.ShapeDtypeStruct((B, tq, H * KV_LORA_RANK), BF16),
        compiler_params=pltpu.CompilerParams(
            dimension_semantics=("arbitrary",), vmem_limit_bytes=VMEM_LIMIT),
        name="attn_sample",
    )(q, cache_kv, cache_kr, k_new)


def _post_kernel(x_ref, olat_ref, opool_ref, wuv_ref, gattn_ref, wo_ref, g2_ref, wg_ref, wu_ref, wd_ref,
                 gf_ref, y_ref):
    pair_w = 2 * KV_LORA_RANK
    o_attn = jnp.concatenate(
        [jnp.dot(olat_ref[0, :, i * pair_w:(i + 1) * pair_w], wuv_ref[i], preferred_element_type=F32)
         for i in range(wuv_ref.shape[0])], axis=1)
    a_n = _rms(o_attn, gattn_ref[...]).astype(BF16)
    merged = jnp.concatenate([a_n, opool_ref[0]], axis=1)
    x1 = x_ref[0] + jnp.dot(merged, wo_ref[...], preferred_element_type=F32)
    h2 = _rms(x1, g2_ref[...]).astype(BF16)
    d_ff = wg_ref.shape[1]
    x2 = x1
    for lo in range(0, d_ff, FFN_CHUNK):
        hi = min(lo + FFN_CHUNK, d_ff)
        gate = jnp.dot(h2, wg_ref[:, lo:hi], preferred_element_type=F32)
        up = jnp.dot(h2, wu_ref[:, lo:hi], preferred_element_type=F32)
        act = (gate * jax.nn.sigmoid(gate) * up).astype(BF16)
        x2 = x2 + jnp.dot(act, wd_ref[lo:hi, :], preferred_element_type=F32)
    y_ref[0] = _rms(x2, gf_ref[...])


def _post_call(x, o_lat, o_pool, wts, *, tm):
    B, T, D = x.shape
    assert T % tm == 0
    in_specs = [
        pl.BlockSpec((1, tm, D), lambda b, t: (b, t, 0)),
        pl.BlockSpec((1, tm, o_lat.shape[-1]), lambda b, t: (b, t, 0)),
        pl.BlockSpec((1, tm, o_pool.shape[-1]), lambda b, t: (b, t, 0)),
    ] + [_const_spec(w.shape) for w in wts]
    return pl.pallas_call(
        _post_kernel,
        grid=(B, T // tm),
        in_specs=in_specs,
        out_specs=pl.BlockSpec((1, tm, D), lambda b, t: (b, t, 0)),
        out_shape=jax.ShapeDtypeStruct((B, T, D), F32),
        compiler_params=pltpu.CompilerParams(
            dimension_semantics=("arbitrary", "arbitrary"), vmem_limit_bytes=VMEM_LIMIT),
        name=f"post_t{tm}",
    )(x, o_lat, o_pool, *wts)


def _swap_halves(w):
    half = w.shape[-1] // 2
    return jnp.concatenate([-w[..., half:], w[..., :half]], axis=-1)


def _rope_cols(w):
    pad = jnp.zeros(w.shape[:-1] + (LANES - 2 * QK_ROPE_DIM,), w.dtype)
    return jnp.concatenate([w, _swap_halves(w), pad], axis=-1)


def _rope_tables(pos):
    d = QK_ROPE_DIM
    freqs = jnp.power(ROPE_BASE, -jnp.arange(0, d, 2, dtype=F32) / d)
    ang = pos.astype(F32)[:, None] * freqs[None, :]
    pad = jnp.zeros((pos.shape[0], LANES - d), F32)
    cos = jnp.concatenate([jnp.cos(ang), jnp.cos(ang), pad], axis=-1)
    sin = jnp.concatenate([jnp.sin(ang), jnp.sin(ang), pad], axis=-1)
    return cos, sin


def _pool_inv_head(pos0, pool_w):
    pos = pos0 + jnp.arange(POOL_PAD)
    win = jnp.repeat(jnp.asarray(POOL_WINDOWS, jnp.int32), pool_w // len(POOL_WINDOWS))
    return 1.0 / jnp.minimum(pos[:, None] + 1, win[None, :]).astype(F32)


def _layer(x, pos0, prev, attn_fn, pre_w, post_w, *, tm_pre, tm_post):
    T = x.shape[1]
    cos, sin = _rope_tables(pos0 + jnp.arange(T))
    inv_head = _pool_inv_head(pos0, prev.shape[-1])
    q, k, ckv, kr, o_pool, new_pool = _pre_call(x, prev, inv_head, cos, sin, pre_w, tm=tm_pre)
    o_lat = attn_fn(q, k)
    B, _, D = x.shape
    flat = lambda a: a.reshape(1, B * T, a.shape[-1])
    y = _post_call(flat(x), flat(o_lat), flat(o_pool), post_w, tm=tm_post).reshape(B, T, D)
    return y, ckv, kr, new_pool


def kernel(x_prompt, x_sample, cache_kv_latent, cache_k_rope, state_pool, g_norm1, w_in, g_q, w_uq, g_kv,
           w_uk, w_uv, w_pool, pool_scale, g_out_attn, g_out_pool, w_o, g_norm2, w_gate, w_up, w_down,
           g_final):
    depth = w_in.shape[0]
    assert depth == 1
    l = 0
    pool_w = w_pool.shape[1] * w_pool.shape[2]
    a0 = Q_LORA_RANK
    a1 = a0 + KV_LORA_RANK
    a2 = a1 + QK_ROPE_DIM
    win = w_in[l]
    w1 = jnp.concatenate([win[:, :a1], _rope_cols(win[:, a1:a2]), win[:, a2:]], axis=1).astype(BF16)
    wq = w_uq[l].reshape(Q_LORA_RANK, N_HEADS, QK_NOPE_DIM + QK_ROPE_DIM)
    w2 = jnp.concatenate([wq[..., :QK_NOPE_DIM].reshape(Q_LORA_RANK, -1),
                          _rope_cols(wq[..., QK_NOPE_DIM:]).reshape(Q_LORA_RANK, -1)], axis=1).astype(BF16)
    eye = jnp.eye(N_HEADS, dtype=F32)
    w3 = jnp.einsum('hrd,hg->hdgr', w_uk[l], eye).reshape(N_HEADS * QK_NOPE_DIM, N_HEADS * KV_LORA_RANK)
    wuv = jnp.einsum('phrd,hg->phrgd', w_uv[l].reshape(N_HEADS // 2, 2, KV_LORA_RANK, -1),
                     jnp.eye(2, dtype=F32)).reshape(N_HEADS // 2, 2 * KV_LORA_RANK, -1)
    row = lambda g: g.reshape(1, -1).astype(F32)
    pre_w = (row(g_norm1[l]), w1, row(g_q[l]), w2, w3.astype(BF16), row(g_kv[l]), w_pool[l].astype(BF16),
             row(pool_scale[l]), row(g_out_pool[l]))
    post_w = (wuv.astype(BF16), row(g_out_attn[l]), w_o[l].astype(BF16), row(g_norm2[l]),
              w_gate[l].astype(BF16), w_up[l].astype(BF16), w_down[l].astype(BF16), row(g_final))

    B, T, _ = x_prompt.shape
    prev_p = jnp.zeros((B, POOL_PAD, pool_w), F32)
    y_p, ckv_p, kr_p, pool_p = _layer(
        x_prompt, 0, prev_p, functools.partial(_attn_prompt_call, tq=256, tk=512), pre_w, post_w,
        tm_pre=512, tm_post=512)

    past = cache_kv_latent.shape[2]
    Ts = x_sample.shape[1]
    prev_s = jnp.pad(state_pool[l], ((0, 0), (POOL_PAD - POOL_STATE, 0), (0, 0)))
    attn_s = lambda q, k: _attn_sample_call(q, cache_kv_latent[l], cache_k_rope[l], k)
    y_s, ckv_s, kr_s, pool_s = _layer(x_sample, past, prev_s, attn_s, pre_w, post_w, tm_pre=Ts,
                                      tm_post=x_sample.shape[0] * Ts)

    return (y_p, y_s, ckv_p[None], kr_p[None], pool_p[None], ckv_s[None], kr_s[None], pool_s[None])
```

```python
import functools
import math

import jax
import jax.numpy as jnp
from jax import lax
from jax.experimental import pallas as pl
from jax.experimental.pallas import tpu as pltpu

CHUNK = 64
N_HEADS = 8
QK_NOPE_DIM = 64
QK_ROPE_DIM = 32
Q_LORA_RANK = 256
KV_LORA_RANK = 128
ROPE_BASE = 10000.0
POOL_WINDOWS = (2, 4, 8, 16)
POOL_STATE = max(POOL_WINDOWS) - 1
POOL_PAD = POOL_STATE + 1
SM_SCALE = 1.0 / math.sqrt(QK_NOPE_DIM + QK_ROPE_DIM)
Q_SCALE = SM_SCALE * math.log2(math.e)
EPS = 1e-6
ATTN_SLAB = 64
FFN_CHUNK = 1024
LANES = 128
QK_PAD = 2 * LANES
MXU_TILE = 256
ROPE_ROLL = LANES - QK_ROPE_DIM
VMEM_LIMIT = 56 * 1024 * 1024

F32 = jnp.float32
BF16 = jnp.bfloat16


def _rms(x, g):
    ms = jnp.mean(x * x, axis=-1, keepdims=True)
    return x * lax.rsqrt(ms + EPS) * g


def _rope_block(blk, cos, sin):
    return blk * cos + pltpu.roll(blk, ROPE_ROLL, 1) * sin


def _const_spec(shape):
    nd = len(shape)
    return pl.BlockSpec(shape, lambda *_: (0,) * nd, pipeline_mode=pl.Buffered(1))


def _pre_kernel(x_ref, prev_ref, inv_ref, cos_ref, sin_ref, g1_ref, w1_ref, gq_ref, w2_ref, w3_ref,
                gkv_ref, wpool_ref, pscale_ref, gpool_ref,
                q_ref, k_ref, ckv_ref, kr_ref, opool_ref, newpool_ref, ext_ref, *, tm, pool_w):
    t = pl.program_id(1)
    n_t = pl.num_programs(1)
    x = x_ref[0]
    h = _rms(x, g1_ref[...]).astype(BF16)
    z = jnp.dot(h, w1_ref[...], preferred_element_type=F32)
    a0 = Q_LORA_RANK
    a1 = a0 + KV_LORA_RANK
    a2 = a1 + LANES
    cos = cos_ref[...]
    sin = sin_ref[...]

    ckv_n = _rms(z[:, a0:a1], gkv_ref[...])
    k_rot = _rope_block(z[:, a1:a2], cos, sin)
    ckv_ref[0] = ckv_n
    kr_ref[0] = k_rot[:, :QK_ROPE_DIM]
    k_ref[0, :, 0:LANES] = ckv_n.astype(BF16)
    spare = lax.broadcasted_iota(jnp.int32, k_rot.shape, 1) >= QK_ROPE_DIM
    k_ref[0, :, LANES:QK_PAD] = jnp.where(spare, 1.0, k_rot).astype(BF16)

    cqn = _rms(z[:, :a0], gq_ref[...]).astype(BF16)
    q = jnp.dot(cqn, w2_ref[...], preferred_element_type=F32)
    n_nope = N_HEADS * QK_NOPE_DIM
    q_lat = jnp.dot(q[:, :n_nope].astype(BF16), w3_ref[...], preferred_element_type=F32)
    for hh in range(N_HEADS):
        q_ref[0, hh, :, 0:LANES] = (q_lat[:, hh * LANES:(hh + 1) * LANES] * Q_SCALE).astype(BF16)
        blk = q[:, n_nope + hh * LANES:n_nope + (hh + 1) * LANES]
        q_ref[0, hh, :, LANES:QK_PAD] = (_rope_block(blk, cos, sin) * Q_SCALE).astype(BF16)

    u = z[:, a2:]

    @pl.when(t == 0)
    def _():
        ext_ref[0:POOL_PAD, :] = prev_ref[0]

    ext_ref[POOL_PAD:POOL_PAD + tm, :] = u
    mixed = []
    for g, w in enumerate(POOL_WINDOWS):
        lo, hi = g * LANES, (g + 1) * LANES
        s = ext_ref[:, lo:hi]
        k = 1
        while k < w:
            s = s + pltpu.roll(s, k, 0)
            k *= 2
        inv_head = jnp.where(t == 0, inv_ref[:, lo:hi], 1.0 / w)
        head = s[POOL_PAD:2 * POOL_PAD] * inv_head - u[:POOL_PAD, lo:hi]
        body = s[2 * POOL_PAD:] * (1.0 / w) - u[POOL_PAD:, lo:hi]
        pooled = jnp.concatenate([head, body], axis=0).astype(BF16)
        mixed.append(jnp.dot(pooled, wpool_ref[g], preferred_element_type=F32))
    o_pool = jnp.concatenate(mixed, axis=1) * pscale_ref[...]
    opool_ref[0] = _rms(o_pool, gpool_ref[...]).astype(BF16)

    @pl.when(t == n_t - 1)
    def _():
        newpool_ref[0] = ext_ref[tm + 1:tm + POOL_PAD, :]

    ext_ref[0:POOL_PAD, :] = ext_ref[tm:tm + POOL_PAD, :]


def _pre_call(x, prev, inv_head, cos, sin, wts, *, tm):
    B, T, D = x.shape
    pool_w = prev.shape[-1]
    assert T % tm == 0 and T >= POOL_PAD and tm >= 2 * POOL_PAD and pool_w == len(POOL_WINDOWS) * LANES
    grid = (B, T // tm)
    in_specs = [
        pl.BlockSpec((1, tm, D), lambda b, t: (b, t, 0)),
        pl.BlockSpec((1, POOL_PAD, pool_w), lambda b, t: (b, 0, 0)),
        _const_spec(inv_head.shape),
        pl.BlockSpec((tm, LANES), lambda b, t: (t, 0)),
        pl.BlockSpec((tm, LANES), lambda b, t: (t, 0)),
    ] + [_const_spec(w.shape) for w in wts]
    out_shape = (
        jax.ShapeDtypeStruct((B, N_HEADS, T, QK_PAD), BF16),
        jax.ShapeDtypeStruct((B, T, QK_PAD), BF16),
        jax.ShapeDtypeStruct((B, T, KV_LORA_RANK), F32),
        jax.ShapeDtypeStruct((B, T, QK_ROPE_DIM), F32),
        jax.ShapeDtypeStruct((B, T, pool_w), BF16),
        jax.ShapeDtypeStruct((B, POOL_STATE, pool_w), F32),
    )
    out_specs = (
        pl.BlockSpec((1, N_HEADS, tm, QK_PAD), lambda b, t: (b, 0, t, 0)),
        pl.BlockSpec((1, tm, QK_PAD), lambda b, t: (b, t, 0)),
        pl.BlockSpec((1, tm, KV_LORA_RANK), lambda b, t: (b, t, 0)),
        pl.BlockSpec((1, tm, QK_ROPE_DIM), lambda b, t: (b, t, 0)),
        pl.BlockSpec((1, tm, pool_w), lambda b, t: (b, t, 0)),
        pl.BlockSpec((1, POOL_STATE, pool_w), lambda b, t: (b, 0, 0)),
    )
    return pl.pallas_call(
        functools.partial(_pre_kernel, tm=tm, pool_w=pool_w),
        grid=grid, in_specs=in_specs, out_specs=out_specs, out_shape=out_shape,
        scratch_shapes=[pltpu.VMEM((POOL_PAD + tm, pool_w), F32)],
        compiler_params=pltpu.CompilerParams(
            dimension_semantics=("arbitrary", "arbitrary"), vmem_limit_bytes=VMEM_LIMIT),
        name=f"pre_t{tm}",
    )(x, prev, inv_head, cos, sin, *wts)


def _attn_prompt_kernel(q_ref, k_ref, o_ref, s0_ref, s1_ref, mc0_ref, mc1_ref, p_ref, alpha_ref, m_ref, acc_ref,
                        *, tq, tk):
    qi = pl.program_id(1)
    rows = N_HEADS * tq
    nt = (((1,), (1,)), ((), ()))
    n_full = (qi * tq) // tk
    slabs = [slice(i * ATTN_SLAB, (i + 1) * ATTN_SLAB) for i in range(rows // ATTN_SLAB)]

    def lane_parts(s):
        return [s[:, c * LANES:(c + 1) * LANES] for c in range(s.shape[1] // LANES)]

    def keys(j, width):
        return k_ref[0, pl.ds(pl.multiple_of(j * tk, tk), width), :]

    def scores(j, buf, masked, width=tk):
        s_ref, mc_ref = buf
        s_all = lax.dot_general(q_ref[0].reshape(rows, QK_PAD), keys(j, width), nt,
                                preferred_element_type=F32)
        n_groups = width // MXU_TILE
        for g in range(n_groups):
            cols = slice(g * MXU_TILE, (g + 1) * MXU_TILE)
            for i, r in enumerate(slabs):
                s = s_all[r, cols]
                if masked:
                    tok = qi * tq + (i * ATTN_SLAB) % tq + lax.broadcasted_iota(jnp.int32, s.shape, 0)
                    key = j * tk + g * MXU_TILE + lax.broadcasted_iota(jnp.int32, s.shape, 1)
                    s = jnp.where(key // CHUNK <= tok // CHUNK, s, -jnp.inf)
                s_ref[r, cols] = s
                part_max = functools.reduce(jnp.maximum, lane_parts(s))
                if g > 0:
                    part_max = jnp.maximum(part_max, mc_ref[r, :])
                if g < n_groups - 1:
                    mc_ref[r, :] = part_max
                else:
                    m_cur = jnp.max(part_max, axis=-1, keepdims=True)
                    mc_ref[r, :] = jnp.broadcast_to(m_cur, (ATTN_SLAB, LANES))

    def softmax_pv(j, buf, width=tk):
        s_ref, mc_ref = buf
        for r in slabs:
            m_prev = m_ref[r, :]
            m_new = jnp.maximum(m_prev, mc_ref[r, :])
            alpha_ref[r, :] = jnp.exp2(m_prev - m_new)
            m_ref[r, :] = m_new
            for c, part in enumerate(lane_parts(s_ref[r, 0:width])):
                p_ref[r, c * LANES:(c + 1) * LANES] = jnp.exp2(part - m_new).astype(BF16)
        alpha = alpha_ref[...]
        pv = jnp.dot(p_ref[:, 0:width], keys(j, width), preferred_element_type=F32)
        for c in range(QK_PAD // LANES):
            cols = slice(c * LANES, (c + 1) * LANES)
            acc_ref[:, cols] = alpha * acc_ref[:, cols] + pv[:, cols]

    m_ref[...] = jnp.full(m_ref.shape, -jnp.inf, F32)
    acc_ref[...] = jnp.zeros(acc_ref.shape, F32)

    bufs = ((s0_ref, mc0_ref), (s1_ref, mc1_ref))
    tail = (qi + 1) * tq - n_full * tk
    tail_widths = range(tq, tk + 1, tq)

    for w in tail_widths:
        @pl.when((n_full == 0) & (tail == w))
        def _():
            scores(0, bufs[0], True, w)
            softmax_pv(0, bufs[0], w)

    @pl.when(n_full > 0)
    def _():
        scores(0, bufs[0], False)

    def pair(i, carry):
        j = 2 * i
        scores(j + 1, bufs[1], False)
        softmax_pv(j, bufs[0])
        scores(j + 2, bufs[0], False)
        softmax_pv(j + 1, bufs[1])
        return carry

    n_pairs = jnp.maximum(n_full - 1, 0) // 2
    lax.fori_loop(0, n_pairs, pair, 0)

    for w in tail_widths:
        @pl.when((n_full % 2 == 1) & (tail == w))
        def _():
            scores(n_full, bufs[1], True, w)
            softmax_pv(n_full - 1, bufs[0])
            softmax_pv(n_full, bufs[1], w)

    @pl.when((n_full % 2 == 0) & (n_full > 0))
    def _():
        scores(n_full - 1, bufs[1], False)
        softmax_pv(n_full - 2, bufs[0])

    for w in tail_widths:
        @pl.when((n_full % 2 == 0) & (n_full > 1) & (tail == w))
        def _():
            scores(n_full, bufs[0], True, w)
            softmax_pv(n_full - 1, bufs[1])

        @pl.when(((n_full + 1) % 2 == 1) & (n_full > 0) & (tail == w))
        def _():
            softmax_pv(n_full, bufs[0], w)

    l = acc_ref[:, QK_PAD - 1:QK_PAD]
    o = (acc_ref[:, 0:KV_LORA_RANK] / l).astype(BF16)
    for hh in range(N_HEADS):
        o_ref[0, :, hh * LANES:(hh + 1) * LANES] = o[hh * tq:(hh + 1) * tq]


def _attn_prompt_call(q, k, *, tq, tk):
    B, H, T, _ = q.shape
    assert T % tq == 0 and T % tk == 0 and tq % CHUNK == 0 and tq % ATTN_SLAB == 0 and tq % MXU_TILE == 0
    rows = H * tq
    return pl.pallas_call(
        functools.partial(_attn_prompt_kernel, tq=tq, tk=tk),
        grid=(B, T // tq),
        in_specs=[
            pl.BlockSpec((1, H, tq, QK_PAD), lambda b, i: (b, 0, i, 0)),
            pl.BlockSpec((1, T, QK_PAD), lambda b, i: (b, 0, 0)),
        ],
        out_specs=pl.BlockSpec((1, tq, H * KV_LORA_RANK), lambda b, i: (b, i, 0)),
        out_shape=jax.ShapeDtypeStruct((B, T, H * KV_LORA_RANK), BF16),
        scratch_shapes=[pltpu.VMEM((rows, tk), F32), pltpu.VMEM((rows, tk), F32),
                        pltpu.VMEM((rows, LANES), F32), pltpu.VMEM((rows, LANES), F32),
                        pltpu.VMEM((rows, tk), BF16),
                        pltpu.VMEM((rows, LANES), F32),
                        pltpu.VMEM((rows, LANES), F32),
                        pltpu.VMEM((rows, QK_PAD), F32)],
        compiler_params=pltpu.CompilerParams(
            dimension_semantics=("arbitrary", "arbitrary"), vmem_limit_bytes=VMEM_LIMIT),
        name="attn_prompt",
    )(q, k)


def _attn_sample_kernel(q_ref, ckv_ref, ckr_ref, knew_ref, o_ref, *, tq, past):
    rows = N_HEADS * tq
    q = q_ref[0].reshape(rows, QK_PAD)
    q_lat = q[:, :KV_LORA_RANK]
    q_rope = q[:, KV_LORA_RANK:KV_LORA_RANK + QK_ROPE_DIM]
    nt = (((1,), (1,)), ((), ()))
    v_past = ckv_ref[0].astype(BF16)
    s_past = (lax.dot_general(q_lat, v_past, nt, preferred_element_type=F32)
              + lax.dot_general(q_rope, ckr_ref[0].astype(BF16), nt, preferred_element_type=F32))
    k_new = knew_ref[0]
    s_new = lax.dot_general(q, k_new, nt, preferred_element_type=F32)

    def masked(s, k_start):
        q_pos = past + lax.broadcasted_iota(jnp.int32, s.shape, 0) % tq
        k_pos = k_start + lax.broadcasted_iota(jnp.int32, s.shape, 1)
        return jnp.where(k_pos // CHUNK <= q_pos // CHUNK, s, -jnp.inf)

    s_past = masked(s_past, 0)
    s_new = masked(s_new, past)
    m = jnp.maximum(jnp.max(s_past, axis=-1, keepdims=True), jnp.max(s_new, axis=-1, keepdims=True))
    p_past = jnp.exp2(s_past - m)
    p_new = jnp.exp2(s_new - m)
    l = jnp.sum(p_past, axis=-1, keepdims=True) + jnp.sum(p_new, axis=-1, keepdims=True)
    acc = (jnp.dot(p_past.astype(BF16), v_past, preferred_element_type=F32)
           + jnp.dot(p_new.astype(BF16), k_new[:, :KV_LORA_RANK], preferred_element_type=F32))
    o = (acc / l).astype(BF16)
    for hh in range(N_HEADS):
        o_ref[0, :, hh * LANES:(hh + 1) * LANES] = o[hh * tq:(hh + 1) * tq]


def _attn_sample_call(q, cache_kv, cache_kr, k_new):
    B, H, tq, _ = q.shape
    past = cache_kv.shape[1]
    return pl.pallas_call(
        functools.partial(_attn_sample_kernel, tq=tq, past=past),
        grid=(B,),
        in_specs=[
            pl.BlockSpec((1, H, tq, QK_PAD), lambda b: (b, 0, 0, 0)),
            pl.BlockSpec((1, past, KV_LORA_RANK), lambda b: (b, 0, 0)),
            pl.BlockSpec((1, past, QK_ROPE_DIM), lambda b: (b, 0, 0)),
            pl.BlockSpec((1, tq, QK_PAD), lambda b: (b, 0, 0)),
        ],
        out_specs=pl.BlockSpec((1, tq, H * KV_LORA_RANK), lambda b: (b, 0, 0)),
        out_shape=jax.ShapeDtypeStruct((B, tq, H * KV_LORA_RANK), BF16),
        compiler_params=pltpu.CompilerParams(
            dimension_semantics=("arbitrary",), vmem_limit_bytes=VMEM_LIMIT),
        name="attn_sample",
    )(q, cache_kv, cache_kr, k_new)


def _post_kernel(x_ref, olat_ref, opool_ref, wuv_ref, gattn_ref, wo_ref, g2_ref, wg_ref, wu_ref, wd_ref,
                 gf_ref, y_ref):
    pair_w = 2 * KV_LORA_RANK
    o_attn = jnp.concatenate(
        [jnp.dot(olat_ref[0, :, i * pair_w:(i + 1) * pair_w], wuv_ref[i], preferred_element_type=F32)
         for i in range(wuv_ref.shape[0])], axis=1)
    a_n = _rms(o_attn, gattn_ref[...]).astype(BF16)
    merged = jnp.concatenate([a_n, opool_ref[0]], axis=1)
    x1 = x_ref[0] + jnp.dot(merged, wo_ref[...], preferred_element_type=F32)
    h2 = _rms(x1, g2_ref[...]).astype(BF16)
    d_ff = wg_ref.shape[1]
    x2 = x1
    for lo in range(0, d_ff, FFN_CHUNK):
        hi = min(lo + FFN_CHUNK, d_ff)
        gate = jnp.dot(h2, wg_ref[:, lo:hi], preferred_element_type=F32)
        up = jnp.dot(h2, wu_ref[:, lo:hi], preferred_element_type=F32)
        act = (gate * jax.nn.sigmoid(gate) * up).astype(BF16)
        x2 = x2 + jnp.dot(act, wd_ref[lo:hi, :], preferred_element_type=F32)
    y_ref[0] = _rms(x2, gf_ref[...])


def _post_call(x, o_lat, o_pool, wts, *, tm):
    B, T, D = x.shape
    assert T % tm == 0
    in_specs = [
        pl.BlockSpec((1, tm, D), lambda b, t: (b, t, 0)),
        pl.BlockSpec((1, tm, o_lat.shape[-1]), lambda b, t: (b, t, 0)),
        pl.BlockSpec((1, tm, o_pool.shape[-1]), lambda b, t: (b, t, 0)),
    ] + [_const_spec(w.shape) for w in wts]
    return pl.pallas_call(
        _post_kernel,
        grid=(B, T // tm),
        in_specs=in_specs,
        out_specs=pl.BlockSpec((1, tm, D), lambda b, t: (b, t, 0)),
        out_shape=jax.ShapeDtypeStruct((B, T, D), F32),
        compiler_params=pltpu.CompilerParams(
            dimension_semantics=("arbitrary", "arbitrary"), vmem_limit_bytes=VMEM_LIMIT),
        name=f"post_t{tm}",
    )(x, o_lat, o_pool, *wts)


def _swap_halves(w):
    half = w.shape[-1] // 2
    return jnp.concatenate([-w[..., half:], w[..., :half]], axis=-1)


def _rope_cols(w):
    pad = jnp.zeros(w.shape[:-1] + (LANES - 2 * QK_ROPE_DIM,), w.dtype)
    return jnp.concatenate([w, _swap_halves(w), pad], axis=-1)


def _rope_tables(pos):
    d = QK_ROPE_DIM
    freqs = jnp.power(ROPE_BASE, -jnp.arange(0, d, 2, dtype=F32) / d)
    ang = pos.astype(F32)[:, None] * freqs[None, :]
    pad = jnp.zeros((pos.shape[0], LANES - d), F32)
    cos = jnp.concatenate([jnp.cos(ang), jnp.cos(ang), pad], axis=-1)
    sin = jnp.concatenate([jnp.sin(ang), jnp.sin(ang), pad], axis=-1)
    return cos, sin


def _pool_inv_head(pos0, pool_w):
    pos = pos0 + jnp.arange(POOL_PAD)
    win = jnp.repeat(jnp.asarray(POOL_WINDOWS, jnp.int32), pool_w // len(POOL_WINDOWS))
    return 1.0 / jnp.minimum(pos[:, None] + 1, win[None, :]).astype(F32)


def _layer(x, pos0, prev, attn_fn, pre_w, post_w, *, tm_pre, tm_post):
    T = x.shape[1]
    cos, sin = _rope_tables(pos0 + jnp.arange(T))
    inv_head = _pool_inv_head(pos0, prev.shape[-1])
    q, k, ckv, kr, o_pool, new_pool = _pre_call(x, prev, inv_head, cos, sin, pre_w, tm=tm_pre)
    o_lat = attn_fn(q, k)
    B, _, D = x.shape
    flat = lambda a: a.reshape(1, B * T, a.shape[-1])
    y = _post_call(flat(x), flat(o_lat), flat(o_pool), post_w, tm=tm_post).reshape(B, T, D)
    return y, ckv, kr, new_pool


def kernel(x_prompt, x_sample, cache_kv_latent, cache_k_rope, state_pool, g_norm1, w_in, g_q, w_uq, g_kv,
           w_uk, w_uv, w_pool, pool_scale, g_out_attn, g_out_pool, w_o, g_norm2, w_gate, w_up, w_down,
           g_final):
    depth = w_in.shape[0]
    assert depth == 1
    l = 0
    pool_w = w_pool.shape[1] * w_pool.shape[2]
    a0 = Q_LORA_RANK
    a1 = a0 + KV_LORA_RANK
    a2 = a1 + QK_ROPE_DIM
    win = w_in[l]
    w1 = jnp.concatenate([win[:, :a1], _rope_cols(win[:, a1:a2]), win[:, a2:]], axis=1).astype(BF16)
    wq = w_uq[l].reshape(Q_LORA_RANK, N_HEADS, QK_NOPE_DIM + QK_ROPE_DIM)
    w2 = jnp.concatenate([wq[..., :QK_NOPE_DIM].reshape(Q_LORA_RANK, -1),
                          _rope_cols(wq[..., QK_NOPE_DIM:]).reshape(Q_LORA_RANK, -1)], axis=1).astype(BF16)
    eye = jnp.eye(N_HEADS, dtype=F32)
    w3 = jnp.einsum('hrd,hg->hdgr', w_uk[l], eye).reshape(N_HEADS * QK_NOPE_DIM, N_HEADS * KV_LORA_RANK)
    wuv = jnp.einsum('phrd,hg->phrgd', w_uv[l].reshape(N_HEADS // 2, 2, KV_LORA_RANK, -1),
                     jnp.eye(2, dtype=F32)).reshape(N_HEADS // 2, 2 * KV_LORA_RANK, -1)
    row = lambda g: g.reshape(1, -1).astype(F32)
    pre_w = (row(g_norm1[l]), w1, row(g_q[l]), w2, w3.astype(BF16), row(g_kv[l]), w_pool[l].astype(BF16),
             row(pool_scale[l]), row(g_out_pool[l]))
    post_w = (wuv.astype(BF16), row(g_out_attn[l]), w_o[l].astype(BF16), row(g_norm2[l]),
              w_gate[l].astype(BF16), w_up[l].astype(BF16), w_down[l].astype(BF16), row(g_final))

    B, T, _ = x_prompt.shape
    prev_p = jnp.zeros((B, POOL_PAD, pool_w), F32)
    y_p, ckv_p, kr_p, pool_p = _layer(
        x_prompt, 0, prev_p, functools.partial(_attn_prompt_call, tq=256, tk=512), pre_w, post_w,
        tm_pre=512, tm_post=512)

    past = cache_kv_latent.shape[2]
    Ts = x_sample.shape[1]
    prev_s = jnp.pad(state_pool[l], ((0, 0), (POOL_PAD - POOL_STATE, 0), (0, 0)))
    attn_s = lambda q, k: _attn_sample_call(q, cache_kv_latent[l], cache_k_rope[l], k)
    y_s, ckv_s, kr_s, pool_s = _layer(x_sample, past, prev_s, attn_s, pre_w, post_w, tm_pre=Ts,
                                      tm_post=x_sample.shape[0] * Ts)

    return (y_p, y_s, ckv_p[None], kr_p[None], pool_p[None], ckv_s[None], kr_s[None], pool_s[None])
```
